```python
import jax, jax.numpy as jnp
from jax import lax
import numpy as np

D_MODEL = 1024
BATCH = 2
SEQ = 16384
DEPTH = 1
DEC_BATCH = 32
DEC_SEQ = 2048
PAST_LEN = 128

HEAD_DIM = 64
MIX_WIDTH = D_MODEL
N_HEADS_A = (MIX_WIDTH // 2) // HEAD_DIM
N_HEADS_B = (MIX_WIDTH // 2) // HEAD_DIM
N_KV_B = N_HEADS_B // 4
DILATED_PATTERNS = ((128, 1), (512, 4), (2048, 16))
SWA_HALF_WINDOW = 128
SWA_BLOCK = 128
ROPE_THETA = 10000.0
D_FF = -(-8 * D_MODEL // (3 * 256)) * 256
EPS = 1e-6

QA = N_HEADS_A * HEAD_DIM
KA = N_HEADS_A * HEAD_DIM
VA = N_HEADS_A * HEAD_DIM
QB = N_HEADS_B * HEAD_DIM
KB = N_KV_B * HEAD_DIM
VB = N_KV_B * HEAD_DIM
IN_COLS = QA + KA + VA + QB + KB + VB

kernel_name = "hymba_dilated_swa_sink_encoder"


def rmsnorm(x, g):
    x32 = x.astype(jnp.float32)
    y = x32 * lax.rsqrt(jnp.mean(x32 * x32, axis=-1, keepdims=True) + EPS)
    return (y * g.astype(jnp.float32)).astype(x.dtype)


def rope(t, positions):
    half = HEAD_DIM // 2
    inv = ROPE_THETA ** (-2.0 * jnp.arange(half, dtype=jnp.float32) / HEAD_DIM)
    ang = positions.astype(jnp.float32)[:, None] * inv[None, :]
    cos = jnp.cos(ang)[None, :, None, :]
    sin = jnp.sin(ang)[None, :, None, :]
    t32 = t.astype(jnp.float32)
    t1, t2 = t32[..., :half], t32[..., half:]
    return jnp.concatenate([t1 * cos - t2 * sin, t2 * cos + t1 * sin], axis=-1).astype(t.dtype)


def banded_attention(q, k, v, half_window, block, sink=None):
    B, L, Hk, G, Dh = q.shape
    nb = -(-L // block)
    Lp = nb * block
    pad = Lp - L
    q32 = jnp.pad(q.astype(jnp.float32), ((0, 0), (0, pad), (0, 0), (0, 0), (0, 0)))
    kp = jnp.pad(k.astype(jnp.float32), ((0, 0), (block, pad + block), (0, 0), (0, 0)))
    vp = jnp.pad(v.astype(jnp.float32), ((0, 0), (block, pad + block), (0, 0), (0, 0)))
    qb = q32.reshape(B, nb, block, Hk, G, Dh)
    kb = kp.reshape(B, nb + 2, block, Hk, Dh)
    vb = vp.reshape(B, nb + 2, block, Hk, Dh)
    kw = jnp.concatenate([kb[:, :-2], kb[:, 1:-1], kb[:, 2:]], axis=2)
    vw = jnp.concatenate([vb[:, :-2], vb[:, 1:-1], vb[:, 2:]], axis=2)
    rel = jnp.arange(3 * block)[None, :] - block - jnp.arange(block)[:, None]
    band = jnp.abs(rel) <= half_window
    kpos = (jnp.arange(nb)[:, None] - 1) * block + jnp.arange(3 * block)[None, :]
    valid = (kpos >= 0) & (kpos < L)
    mask = band[None] & valid[:, None, :]
    scale = HEAD_DIM ** -0.5
    s = jnp.einsum('bnqhgd,bnkhd->bnhgqk', qb, kw) * scale
    s = jnp.where(mask[None, :, None, None], s, -jnp.inf)
    m = jnp.max(s, axis=-1)
    if sink is not None:
        sk = sink.astype(jnp.float32)[None, None, :, :, None]
        m = jnp.maximum(m, sk)
    p = jnp.exp(s - m[..., None])
    denom = jnp.sum(p, axis=-1)
    if sink is not None:
        denom = denom + jnp.exp(sk - m)
    o = jnp.einsum('bnhgqk,bnkhd->bnqhgd', p, vw)
    denom_t = denom.transpose(0, 1, 4, 2, 3)
    o = o / denom_t[..., None]
    lse = (m.transpose(0, 1, 4, 2, 3) + jnp.log(denom_t))
    o = o.reshape(B, Lp, Hk, G, Dh)[:, :L]
    lse = lse.reshape(B, Lp, Hk, G)[:, :L]
    return o, lse


def dilated_branch(q, k, v, window, dilation):
    B, S, H, Dh = q.shape
    L = S // dilation
    count = (window // 2) // dilation

    def to_sub(t):
        return t.reshape(B, L, dilation, H, Dh).transpose(0, 2, 1, 3, 4).reshape(B * dilation, L, H, Dh)

    o, lse = banded_attention(to_sub(q)[:, :, :, None, :], to_sub(k), to_sub(v), count, count)
    o = o.reshape(B, dilation, L, H, Dh).transpose(0, 2, 1, 3, 4).reshape(B, S, H, Dh)
    lse = lse.reshape(B, dilation, L, H).transpose(0, 2, 1, 3).reshape(B, S, H)
    return o, lse


def encoder_layer(x, attn_norm, w_in, qnorm_a, knorm_a, qnorm_b, knorm_b, sink_b,
                  w_out, ffn_norm, w_gate, w_up, w_down):
    B, S, _ = x.shape
    pos = jnp.arange(S)
    h = rmsnorm(x, attn_norm)
    proj = h @ w_in
    splits = np.cumsum([QA, KA, VA, QB, KB])
    qa, ka, va, qb, kb, vb = jnp.split(proj, splits, axis=-1)
    qa = rope(rmsnorm(qa.reshape(B, S, N_HEADS_A, HEAD_DIM), qnorm_a), pos)
    ka = rope(rmsnorm(ka.reshape(B, S, N_HEADS_A, HEAD_DIM), knorm_a), pos)
    va = va.reshape(B, S, N_HEADS_A, HEAD_DIM)
    qb = rope(rmsnorm(qb.reshape(B, S, N_HEADS_B, HEAD_DIM), qnorm_b), pos)
    kb = rope(rmsnorm(kb.reshape(B, S, N_KV_B, HEAD_DIM), knorm_b), pos)
    vb = vb.reshape(B, S, N_KV_B, HEAD_DIM)

    outs, lses = [], []
    for window, dilation in DILATED_PATTERNS:
        o, l = dilated_branch(qa, ka, va, window, dilation)
        outs.append(o)
        lses.append(l)
    wts = jax.nn.softmax(jnp.stack(lses, axis=0), axis=0)
    out_a = jnp.sum(wts[..., None] * jnp.stack(outs, axis=0), axis=0)
    out_a = out_a.reshape(B, S, QA).astype(x.dtype)

    G = N_HEADS_B // N_KV_B
    out_b, _ = banded_attention(qb.reshape(B, S, N_KV_B, G, HEAD_DIM), kb, vb,
                                SWA_HALF_WINDOW, SWA_BLOCK, sink=sink_b.reshape(N_KV_B, G))
    out_b = out_b.reshape(B, S, QB).astype(x.dtype)

    x = x + jnp.concatenate([out_a, out_b], axis=-1) @ w_out
    h2 = rmsnorm(x, ffn_norm)
    x = x + (jax.nn.silu(h2 @ w_gate) * (h2 @ w_up)) @ w_down
    return x


def setup_inputs(seed: int = 0) -> dict:
    key = jax.random.key(seed)
    ks = jax.random.split(key, 16)
    f32 = jnp.float32

    def nrm(k, shape, scale):
        return jax.random.normal(k, shape, f32) * scale

    return {
        "x_prompt": nrm(ks[0], (BATCH, SEQ, D_MODEL), 1.0),
        "x_sample": nrm(ks[1], (DEC_BATCH, DEC_SEQ, D_MODEL), 1.0),
        "attn_norm": 1.0 + nrm(ks[2], (DEPTH, D_MODEL), 0.02),
        "w_in": nrm(ks[3], (DEPTH, D_MODEL, IN_COLS), D_MODEL ** -0.5),
        "qnorm_a": 1.0 + nrm(ks[4], (DEPTH, HEAD_DIM), 0.02),
        "knorm_a": 1.0 + nrm(ks[5], (DEPTH, HEAD_DIM), 0.02),
        "qnorm_b": 1.0 + nrm(ks[6], (DEPTH, HEAD_DIM), 0.02),
        "knorm_b": 1.0 + nrm(ks[7], (DEPTH, HEAD_DIM), 0.02),
        "sink_b": nrm(ks[8], (DEPTH, N_HEADS_B), 0.5),
        "w_out": nrm(ks[9], (DEPTH, QA + QB, D_MODEL), (QA + QB) ** -0.5),
        "ffn_norm": 1.0 + nrm(ks[10], (DEPTH, D_MODEL), 0.02),
        "w_gate": nrm(ks[11], (DEPTH, D_MODEL, D_FF), D_MODEL ** -0.5),
        "w_up": nrm(ks[12], (DEPTH, D_MODEL, D_FF), D_MODEL ** -0.5),
        "w_down": nrm(ks[13], (DEPTH, D_FF, D_MODEL), D_FF ** -0.5),
    }


def reference(x_prompt, x_sample, attn_norm, w_in, qnorm_a, knorm_a, qnorm_b, knorm_b,
              sink_b, w_out, ffn_norm, w_gate, w_up, w_down):
    y_prompt = x_prompt
    y_sample = x_sample
    for i in range(DEPTH):
        params = (attn_norm[i], w_in[i], qnorm_a[i], knorm_a[i], qnorm_b[i], knorm_b[i],
                  sink_b[i], w_out[i], ffn_norm[i], w_gate[i], w_up[i], w_down[i])
        y_prompt = encoder_layer(y_prompt, *params)
        y_sample = encoder_layer(y_sample, *params)
    return (y_prompt, y_sample)
```

```python
import functools

import jax
import jax.numpy as jnp
from jax import lax
from jax.experimental import pallas as pl
from jax.experimental.pallas import tpu as pltpu

HEAD_DIM = 64
N_HEADS_A = 8
N_HEADS_B = 8
N_KV_B = 2
DILATED_PATTERNS = ((128, 1), (512, 4), (2048, 16))
SWA_HALF_WINDOW = 128
ROPE_THETA = 10000.0
EPS = 1e-6

QA = N_HEADS_A * HEAD_DIM
QB = N_HEADS_B * HEAD_DIM
KB = N_KV_B * HEAD_DIM
LANES = 128
MXU_COLS = 256
NEG_BIG = -1e30
VMEM_LIMIT_BYTES = 56 * 1024 * 1024

BF16 = jnp.bfloat16
F32 = jnp.float32


def _pick_tile(n, cap):
    t = cap
    while n % t:
        t //= 2
    return t


def _const_spec(shape):
    return pl.BlockSpec(shape, lambda *_: (0,) * len(shape), pipeline_mode=pl.Buffered(1))


def _in_proj_kernel(x_ref, g_ref, w_ref, cos_ref, sin_ref, gains_ref,
                    qa_ref, ka_ref, va_ref, qb_ref, kb_ref, vb_ref):
    tm = x_ref.shape[0]
    x = x_ref[...]
    ms = jnp.mean(x * x, axis=-1, keepdims=True)
    h = (x * lax.rsqrt(ms + EPS) * g_ref[...]).astype(BF16)

    r = lax.broadcasted_iota(jnp.int32, (LANES, LANES), 0)
    c = lax.broadcasted_iota(jnp.int32, (LANES, LANES), 1)
    ones_blk = ((r // HEAD_DIM) == (c // HEAD_DIM)).astype(BF16)
    lane = lax.broadcasted_iota(jnp.int32, (tm, LANES), 1)
    first_half = (lane % HEAD_DIM) < (HEAD_DIM // 2)

    cos = cos_ref[...]
    sin = sin_ref[...]
    rope_a = [gains_ref[k:k + 1, :] * cos for k in range(4)]
    rope_b = [gains_ref[4 + k:5 + k, :] * sin for k in range(4)]

    def norm_rope(p, kind):
        ss = jnp.dot((p * p).astype(BF16), ones_blk, preferred_element_type=F32)
        rinv = lax.rsqrt(ss * (1.0 / HEAD_DIM) + EPS)
        swapped = jnp.where(first_half,
                            pltpu.roll(p, LANES - HEAD_DIM // 2, 1),
                            pltpu.roll(p, HEAD_DIM // 2, 1))
        return (rinv * (p * rope_a[kind] + swapped * rope_b[kind])).astype(BF16)

    plan = ([(qa_ref, i, 0) for i in range(4)] + [(ka_ref, i, 1) for i in range(4)]
            + [(va_ref, i, None) for i in range(4)] + [(qb_ref, i, 2) for i in range(4)]
            + [(kb_ref, 0, 3), (vb_ref, 0, None)])
    n_chunks = w_ref.shape[1] // MXU_COLS
    for ch in range(n_chunks):
        proj = jnp.dot(h, w_ref[:, ch * MXU_COLS:(ch + 1) * MXU_COLS], preferred_element_type=F32)
        for half in range(MXU_COLS // LANES):
            out_ref, blk, kind = plan[ch * (MXU_COLS // LANES) + half]
            p = proj[:, half * LANES:(half + 1) * LANES]
            val = p.astype(BF16) if kind is None else norm_rope(p, kind)
            out_ref[:, blk * LANES:(blk + 1) * LANES] = val


def _in_proj(x, attn_norm, w_in_bf16, cos_t, sin_t, gains):
    b, s, d = x.shape
    tm = _pick_tile(s, 512)
    tok = lambda cols: pl.BlockSpec((None, tm, cols), lambda i, n: (i, n, 0))
    tab = pl.BlockSpec((tm, LANES), lambda i, n: (n, 0))
    out_cols = (QA, QA, QA, QB, KB, KB)
    return pl.pallas_call(
        _in_proj_kernel,
        grid=(b, s // tm),
        in_specs=[tok(d), _const_spec((1, d)), _const_spec(w_in_bf16.shape), tab, tab,
                  _const_spec(gains.shape)],
        out_specs=[tok(cc) for cc in out_cols],
        out_shape=[jax.ShapeDtypeStruct((b, s, cc), BF16) for cc in out_cols],
        compiler_params=pltpu.CompilerParams(
            dimension_semantics=("parallel", "parallel"), vmem_limit_bytes=VMEM_LIMIT_BYTES),
        name="in_proj",
    )(x, attn_norm.reshape(1, d), w_in_bf16, cos_t, sin_t, gains)


def _fill_window(buf, left_ref, mid_ref, right_ref, w):
    t = mid_ref.shape[0]
    buf[0:w] = left_ref[...]
    buf[w:w + t] = mid_ref[...]
    buf[w + t:w + t + w] = right_ref[...]


def _band_bias(w, lo_col, hi_col):
    a = lax.broadcasted_iota(jnp.int32, (w, 3 * w), 0)
    c = lax.broadcasted_iota(jnp.int32, (w, 3 * w), 1)
    ok = (c >= a) & (c <= a + 2 * w) & (c >= lo_col) & (c < hi_col)
    return jnp.where(ok, 0.0, NEG_BIG).astype(F32)


def _pair_scores(qp, kp, bias, lo_q):
    zero = jnp.zeros_like(qp)
    qs = jnp.concatenate([jnp.where(lo_q, qp, zero), jnp.where(lo_q, zero, qp)], axis=0)
    s = lax.dot_general(qs, kp, (((1,), (1,)), ((), ())), preferred_element_type=F32)
    return s + jnp.concatenate([bias, bias], axis=0)


def _dilated_attn_kernel(q_ref, kl_ref, km_ref, kr_ref, vl_ref, vm_ref, vr_ref,
                         o_ref, lse_ref, kbuf, vbuf, *, w, n_tiles):
    tq = q_ref.shape[0]
    nblk = tq // w
    n = pl.program_id(2)
    _fill_window(kbuf, kl_ref, km_ref, kr_ref, w)
    _fill_window(vbuf, vl_ref, vm_ref, vr_ref, w)
    lo_q = lax.broadcasted_iota(jnp.int32, (w, LANES), 1) < HEAD_DIM
    lo_v = lax.broadcasted_iota(jnp.int32, (3 * w, LANES), 1) < HEAD_DIM
    lane = lax.broadcasted_iota(jnp.int32, (w, LANES), 1)

    def block(j, carry):
        r0 = pl.multiple_of(j * w, w)
        lo_col = jnp.where((n == 0) & (j == 0), w, 0)
        hi_col = jnp.where((n == n_tiles - 1) & (j == nblk - 1), 2 * w, 3 * w)
        bias = _band_bias(w, lo_col, hi_col)
        lse_tile = jnp.zeros((w, LANES), F32)
        for p in range(N_HEADS_A // 2):
            cs = slice(p * LANES, (p + 1) * LANES)
            kp = kbuf[pl.ds(r0, 3 * w), cs]
            vp = vbuf[pl.ds(r0, 3 * w), cs]
            s = _pair_scores(q_ref[pl.ds(r0, w), cs], kp, bias, lo_q)
            m = jnp.max(s, axis=-1, keepdims=True)
            pr = jnp.exp(s - m)
            l = jnp.sum(pr, axis=-1, keepdims=True)
            pb = pr.astype(BF16)
            zero = jnp.zeros_like(vp)
            o_even = jnp.dot(pb[:w], jnp.where(lo_v, vp, zero), preferred_element_type=F32)
            o_odd = jnp.dot(pb[w:], jnp.where(lo_v, zero, vp), preferred_element_type=F32)
            inv = 1.0 / l
            o_ref[pl.ds(r0, w), cs] = (o_even * inv[:w] + o_odd * inv[w:]).astype(BF16)
            lse = m + jnp.log(l)
            lse_tile = (lse_tile + jnp.where(lane == 2 * p, lse[:w], 0.0)
                        + jnp.where(lane == 2 * p + 1, lse[w:], 0.0))
        lse_ref[pl.ds(r0, w), :] = lse_tile
        return carry

    lax.fori_loop(0, nblk, block, 0)


def _window_specs(tq, w, cols, n_halo_blocks, place):
    per = tq // w
    left = pl.BlockSpec((None, w, cols), lambda *g: place(g, jnp.maximum(g[-1] * per - 1, 0)))
    mid = pl.BlockSpec((None, tq, cols), lambda *g: place(g, g[-1]))
    right = pl.BlockSpec((None, w, cols),
                         lambda *g: place(g, jnp.minimum((g[-1] + 1) * per, n_halo_blocks - 1)))
    return [left, mid, right]


def _dilated_attn(q, k, v, dilation, half_window):
    b, s, cols = q.shape
    w = half_window
    ln = s // dilation
    tq = _pick_tile(ln, 512)
    n_tiles = ln // tq
    view = lambda t: t.reshape(b, ln, dilation * t.shape[-1])
    kv_specs = _window_specs(tq, w, cols, ln // w, lambda g, row_block: (g[0], row_block, g[1]))
    o, lse = pl.pallas_call(
        functools.partial(_dilated_attn_kernel, w=w, n_tiles=n_tiles),
        grid=(b, dilation, n_tiles),
        in_specs=[pl.BlockSpec((None, tq, cols), lambda i, r, n: (i, n, r))] + kv_specs + kv_specs,
        out_specs=[pl.BlockSpec((None, tq, cols), lambda i, r, n: (i, n, r)),
                   pl.BlockSpec((None, tq, LANES), lambda i, r, n: (i, n, r))],
        out_shape=[jax.ShapeDtypeStruct((b, ln, dilation * cols), BF16),
                   jax.ShapeDtypeStruct((b, ln, dilation * LANES), F32)],
        scratch_shapes=[pltpu.VMEM((tq + 2 * w, cols), BF16), pltpu.VMEM((tq + 2 * w, cols), BF16)],
        compiler_params=pltpu.CompilerParams(
            dimension_semantics=("parallel", "parallel", "parallel"),
            vmem_limit_bytes=VMEM_LIMIT_BYTES),
        name=f"dilated_attn_d{dilation}",
    )(view(q), view(k), view(k), view(k), view(v), view(v), view(v))
    return o.reshape(b, s, cols), lse.reshape(b, s, LANES)


def _swa_kernel(sink_ref, q_ref, kl_ref, km_ref, kr_ref, vl_ref, vm_ref, vr_ref,
                o_ref, kbuf, vbuf, *, w, n_tiles):
    tq = q_ref.shape[0]
    nblk = tq // w
    n = pl.program_id(1)
    _fill_window(kbuf, kl_ref, km_ref, kr_ref, w)
    _fill_window(vbuf, vl_ref, vm_ref, vr_ref, w)
    lo_q = lax.broadcasted_iota(jnp.int32, (w, LANES), 1) < HEAD_DIM
    lo_v = lax.broadcasted_iota(jnp.int32, (3 * w, LANES), 1) < HEAD_DIM
    upper_rows = lax.broadcasted_iota(jnp.int32, (2 * w, 1), 0) >= w
    group = N_HEADS_B // N_KV_B

    def block(j, carry):
        r0 = pl.multiple_of(j * w, w)
        lo_col = jnp.where((n == 0) & (j == 0), w, 0)
        hi_col = jnp.where((n == n_tiles - 1) & (j == nblk - 1), 2 * w, 3 * w)
        bias = _band_bias(w, lo_col, hi_col)
        kp = kbuf[pl.ds(r0, 3 * w), :]
        vp = vbuf[pl.ds(r0, 3 * w), :]
        zero = jnp.zeros_like(vp)
        v_lo = jnp.where(lo_v, vp, zero)
        v_hi = jnp.where(lo_v, zero, vp)
        for p in range(group):
            cs = slice(p * LANES, (p + 1) * LANES)
            s = _pair_scores(q_ref[pl.ds(r0, w), cs], kp, bias, lo_q)
            sink = jnp.where(upper_rows, sink_ref[group + p], sink_ref[p])
            m = jnp.maximum(jnp.max(s, axis=-1, keepdims=True), sink)
            pr = jnp.exp(s - m)
            l = jnp.sum(pr, axis=-1, keepdims=True) + jnp.exp(sink - m)
            pb = pr.astype(BF16)
            o_lo = jnp.dot(pb[:w], v_lo, preferred_element_type=F32)
            o_hi = jnp.dot(pb[w:], v_hi, preferred_element_type=F32)
            inv = 1.0 / l
            o_ref[pl.ds(r0, w), cs] = (o_lo * inv[:w] + o_hi * inv[w:]).astype(BF16)
        return carry

    lax.fori_loop(0, nblk, block, 0)


def _swa_attn(q, k, v, sink, half_window):
    b, s, cols = q.shape
    w = half_window
    tq = _pick_tile(s, 512)
    n_tiles = s // tq
    kv_specs = _window_specs(tq, w, KB, s // w, lambda g, row_block: (g[0], row_block, 0))
    return pl.pallas_call(
        functools.partial(_swa_kernel, w=w, n_tiles=n_tiles),
        grid=(b, n_tiles),
        in_specs=[pl.BlockSpec(memory_space=pltpu.SMEM),
                  pl.BlockSpec((None, tq, cols), lambda i, n: (i, n, 0))] + kv_specs + kv_specs,
        out_specs=pl.BlockSpec((None, tq, cols), lambda i, n: (i, n, 0)),
        out_shape=jax.ShapeDtypeStruct((b, s, cols), BF16),
        scratch_shapes=[pltpu.VMEM((tq + 2 * w, KB), BF16), pltpu.VMEM((tq + 2 * w, KB), BF16)],
        compiler_params=pltpu.CompilerParams(
            dimension_semantics=("parallel", "parallel"), vmem_limit_bytes=VMEM_LIMIT_BYTES),
        name="swa_attn",
    )(sink, q, k, k, k, v, v, v)


def _out_ffn_kernel(x_ref, o1_ref, o2_ref, o3_ref, l1_ref, l2_ref, l3_ref, ob_ref,
                    expand_ref, wo_ref, g_ref, wg_ref, wu_ref, wd_ref, y_ref, acc_ref, *, ff_chunk):
    l1, l2, l3 = l1_ref[...], l2_ref[...], l3_ref[...]
    mx = jnp.maximum(jnp.maximum(l1, l2), l3)
    e1, e2, e3 = jnp.exp(l1 - mx), jnp.exp(l2 - mx), jnp.exp(l3 - mx)
    inv = 1.0 / (e1 + e2 + e3)

    def expand(wt):
        hi = wt.astype(BF16)
        lo = (wt - hi.astype(F32)).astype(BF16)
        return (jnp.dot(hi, expand_ref[...], preferred_element_type=F32)
                + jnp.dot(lo, expand_ref[...], preferred_element_type=F32))

    out_a = (expand(e1 * inv) * o1_ref[...].astype(F32)
             + expand(e2 * inv) * o2_ref[...].astype(F32)
             + expand(e3 * inv) * o3_ref[...].astype(F32)).astype(BF16)
    x1 = (x_ref[...]
          + jnp.dot(out_a, wo_ref[0:QA, :], preferred_element_type=F32)
          + jnp.dot(ob_ref[...], wo_ref[QA:QA + QB, :], preferred_element_type=F32))
    ms = jnp.mean(x1 * x1, axis=-1, keepdims=True)
    h2 = (x1 * lax.rsqrt(ms + EPS) * g_ref[...]).astype(BF16)
    acc_ref[...] = x1

    def ff_chunk_step(ch, carry):
        c0 = pl.multiple_of(ch * ff_chunk, ff_chunk)
        gate = jnp.dot(h2, wg_ref[:, pl.ds(c0, ff_chunk)], preferred_element_type=F32)
        up = jnp.dot(h2, wu_ref[:, pl.ds(c0, ff_chunk)], preferred_element_type=F32)
        act = (gate / (1.0 + jnp.exp(-gate)) * up).astype(BF16)
        acc_ref[...] += jnp.dot(act, wd_ref[pl.ds(c0, ff_chunk), :], preferred_element_type=F32)
        return carry

    lax.fori_loop(0, wg_ref.shape[1] // ff_chunk, ff_chunk_step, 0)
    y_ref[...] = acc_ref[...]


def _out_ffn(x, o_branches, lse_branches, ob, expand, wo, ffn_norm, wg, wu, wd):
    b, s, d = x.shape
    tm = _pick_tile(s, 512)
    tok = lambda cols: pl.BlockSpec((None, tm, cols), lambda i, n: (i, n, 0))
    consts = (expand, wo, ffn_norm.reshape(1, d), wg, wu, wd)
    return pl.pallas_call(
        functools.partial(_out_ffn_kernel, ff_chunk=MXU_COLS),
        grid=(b, s // tm),
        in_specs=([tok(d)] + [tok(QA)] * 3 + [tok(LANES)] * 3 + [tok(QB)]
                  + [_const_spec(t.shape) for t in consts]),
        out_specs=tok(d),
        out_shape=jax.ShapeDtypeStruct((b, s, d), F32),
        scratch_shapes=[pltpu.VMEM((tm, d), F32)],
        compiler_params=pltpu.CompilerParams(
            dimension_semantics=("parallel", "parallel"), vmem_limit_bytes=VMEM_LIMIT_BYTES),
        name="out_ffn",
    )(x, *o_branches, *lse_branches, ob, *consts)


def _swa_head_order():
    group = N_HEADS_B // N_KV_B
    return [h for p in range(group) for h in (p, group + p)]


def _head_cols(heads):
    return jnp.concatenate([jnp.arange(h * HEAD_DIM, (h + 1) * HEAD_DIM) for h in heads])


def _rope_tables(s):
    half = HEAD_DIM // 2
    inv = ROPE_THETA ** (-2.0 * jnp.arange(half, dtype=F32) / HEAD_DIM)
    ang = jnp.arange(s).astype(F32)[:, None] * inv[None, :]
    cos, sin = jnp.cos(ang), jnp.sin(ang)
    reps = LANES // HEAD_DIM
    return (jnp.tile(jnp.concatenate([cos, cos], axis=-1), (1, reps)),
            jnp.tile(jnp.concatenate([-sin, sin], axis=-1), (1, reps)))


def _gain_rows(qnorm_a, knorm_a, qnorm_b, knorm_b):
    half = HEAD_DIM // 2
    scale = HEAD_DIM ** -0.5
    rows = [qnorm_a * scale, knorm_a, qnorm_b * scale, knorm_b]
    swapped = [jnp.concatenate([g[half:], g[:half]]) for g in rows]
    return jnp.stack([jnp.tile(g.astype(F32), LANES // HEAD_DIM) for g in rows + swapped])


def _encoder_layer(x, tables, prm):
    cos_t, sin_t = tables
    qa, ka, va, qb, kb, vb = _in_proj(x, prm["attn_norm"], prm["w_in"], cos_t, sin_t, prm["gains"])
    branches = [_dilated_attn(qa, ka, va, dilation, (window // 2) // dilation)
                for window, dilation in DILATED_PATTERNS]
    ob = _swa_attn(qb, kb, vb, prm["sink"], SWA_HALF_WINDOW)
    return _out_ffn(x, [o for o, _ in branches], [l for _, l in branches], ob,
                    prm["expand"], prm["w_out"], prm["ffn_norm"], prm["w_gate"], prm["w_up"], prm["w_down"])


def kernel(x_prompt, x_sample, attn_norm, w_in, qnorm_a, knorm_a, qnorm_b, knorm_b, sink_b, w_out,
           ffn_norm, w_gate, w_up, w_down):
    depth = w_in.shape[0]
    order = _swa_head_order()
    qb0 = 3 * QA
    in_cols = jnp.concatenate([jnp.arange(qb0), qb0 + _head_cols(order),
                               jnp.arange(qb0 + QB, w_in.shape[-1])])
    out_rows = jnp.concatenate([jnp.arange(QA), QA + _head_cols(order)])
    head = lax.broadcasted_iota(jnp.int32, (LANES, QA), 0)
    col = lax.broadcasted_iota(jnp.int32, (LANES, QA), 1)
    expand = (head == col // HEAD_DIM).astype(BF16)
    tables = {x.shape[1]: _rope_tables(x.shape[1]) for x in (x_prompt, x_sample)}
    ys = [x_prompt, x_sample]
    for i in range(depth):
        prm = dict(
            attn_norm=attn_norm[i], ffn_norm=ffn_norm[i],
            w_in=w_in[i][:, in_cols].astype(BF16),
            w_out=w_out[i][out_rows, :].astype(BF16),
            w_gate=w_gate[i].astype(BF16), w_up=w_up[i].astype(BF16), w_down=w_down[i].astype(BF16),
            gains=_gain_rows(qnorm_a[i], knorm_a[i], qnorm_b[i], knorm_b[i]),
            sink=sink_b[i].astype(F32), expand=expand)
        ys = [_encoder_layer(y, tables[y.shape[1]], prm) for y in ys]
    return tuple(ys)
```

```python
import functools
import math

import jax
import jax.numpy as jnp
from jax import lax
from jax.experimental import pallas as pl
from jax.experimental.pallas import tpu as pltpu

HEAD_DIM = 64
N_HEADS_A = 8
N_HEADS_B = 8
N_KV_B = 2
DILATED_PATTERNS = ((128, 1), (512, 4), (2048, 16))
SWA_HALF_WINDOW = 128
ROPE_THETA = 10000.0
EPS = 1e-6

QA = N_HEADS_A * HEAD_DIM
QB = N_HEADS_B * HEAD_DIM
KB = N_KV_B * HEAD_DIM
LANES = 128
MXU_COLS = 256
Q_BLOCK = 128
NEG_BIG = -1e30
VMEM_LIMIT_BYTES = 56 * 1024 * 1024
LOG2_E = math.log2(math.e)

BF16 = jnp.bfloat16
F32 = jnp.float32


def _pick_tile(n, cap):
    t = cap
    while n % t:
        t //= 2
    return t


def _const_spec(shape):
    return pl.BlockSpec(shape, lambda *_: (0,) * len(shape), pipeline_mode=pl.Buffered(1))


def _lane_blocks(n):
    return [slice(i * LANES, (i + 1) * LANES) for i in range(n)]


def _in_proj_kernel(x_ref, g_ref, w_ref, cos_ref, sin_ref, gains_ref,
                    a1_ref, a4_ref, a16_ref, qb_ref, kb_ref, vb_ref, stage_ref):
    tm = x_ref.shape[0]
    x = x_ref[...]
    ms = jnp.mean(x * x, axis=-1, keepdims=True)
    h = (x * lax.rsqrt(ms + EPS) * g_ref[...]).astype(BF16)

    r = lax.broadcasted_iota(jnp.int32, (LANES, LANES), 0)
    c = lax.broadcasted_iota(jnp.int32, (LANES, LANES), 1)
    ones_blk = ((r // HEAD_DIM) == (c // HEAD_DIM)).astype(BF16)
    lane = lax.broadcasted_iota(jnp.int32, (tm, LANES), 1)
    first_half = (lane % HEAD_DIM) < (HEAD_DIM // 2)

    cos = cos_ref[...]
    sin = sin_ref[...]
    rope_a = [gains_ref[k:k + 1, :] * cos for k in range(4)]
    rope_b = [gains_ref[4 + k:5 + k, :] * sin for k in range(4)]

    def norm_rope(p, kind):
        ss = jnp.dot((p * p).astype(BF16), ones_blk, preferred_element_type=F32)
        rinv = lax.rsqrt(ss * (1.0 / HEAD_DIM) + EPS)
        swapped = jnp.where(first_half,
                            pltpu.roll(p, LANES - HEAD_DIM // 2, 1),
                            pltpu.roll(p, HEAD_DIM // 2, 1))
        return rinv * (p * rope_a[kind] + swapped * rope_b[kind])

    n_a = 3 * QA // LANES
    kinds = [0] * 4 + [1] * 4 + [None] * 4 + [2] * 4 + [3, None]
    b_outs = [(qb_ref, i) for i in range(4)] + [(kb_ref, 0), (vb_ref, 0)]
    per_chunk = MXU_COLS // LANES
    for ch in range(w_ref.shape[1] // MXU_COLS):
        proj = jnp.dot(h, w_ref[:, ch * MXU_COLS:(ch + 1) * MXU_COLS], preferred_element_type=F32)
        for half in range(per_chunk):
            blk = ch * per_chunk + half
            p = proj[:, half * LANES:(half + 1) * LANES]
            val = p if kinds[blk] is None else norm_rope(p, kinds[blk])
            if blk >= n_a:
                out_ref, i = b_outs[blk - n_a]
                out_ref[:, i * LANES:(i + 1) * LANES] = val.astype(BF16)
                continue
            a1_ref[:, blk * LANES:(blk + 1) * LANES] = val.astype(BF16)
            stage_ref[blk] = val
            for d, out_ref in ((4, a4_ref), (16, a16_ref)):
                for res in range(d):
                    rows = stage_ref[blk, pl.ds(res, tm // d, stride=d), :]
                    c0 = res * 3 * QA + blk * LANES
                    out_ref[:, c0:c0 + LANES] = rows.astype(BF16)


def _in_proj(x, attn_norm, w_in_bf16, cos_t, sin_t, gains):
    b, s, d = x.shape
    tm = _pick_tile(s, 512)
    n_a = 3 * QA
    tok = lambda rows, cols: pl.BlockSpec((None, rows, cols), lambda i, n: (i, n, 0))
    tab = pl.BlockSpec((tm, LANES), lambda i, n: (n, 0))
    outs = [(tm // dl, s // dl, dl * n_a) for dl in (1, 4, 16)] + [(tm, s, QB), (tm, s, KB), (tm, s, KB)]
    return pl.pallas_call(
        _in_proj_kernel,
        grid=(b, s // tm),
        in_specs=[tok(tm, d), _const_spec((1, d)), _const_spec(w_in_bf16.shape), tab, tab,
                  _const_spec(gains.shape)],
        out_specs=[tok(rows, cols) for rows, _, cols in outs],
        out_shape=[jax.ShapeDtypeStruct((b, ln, cols), BF16) for _, ln, cols in outs],
        scratch_shapes=[pltpu.VMEM((n_a // LANES, tm, LANES), F32)],
        compiler_params=pltpu.CompilerParams(
            dimension_semantics=("parallel", "parallel"), vmem_limit_bytes=VMEM_LIMIT_BYTES),
        name="in_proj",
    )(x, attn_norm.reshape(1, d), w_in_bf16, cos_t, sin_t, gains)


def _band_bias_t(w):
    nk = Q_BLOCK + 2 * w
    cc = jnp.arange(nk)[:, None]
    aa = jnp.arange(Q_BLOCK)[None, :]
    band = (cc >= aa) & (cc <= aa + 2 * w)
    variants = [band & ((cc >= w) | (e & 1 == 0)) & ((cc < nk - w) | (e & 2 == 0)) for e in range(4)]
    return jnp.where(jnp.stack(variants), 0.0, NEG_BIG).astype(BF16)


def _fill_windows(kbuf, vlo_buf, vhi_buf, k_refs, v_refs, w):
    tq = k_refs[1].shape[0]
    for r0, nrows, kr, vr in ((0, w, k_refs[0], v_refs[0]), (w, tq, k_refs[1], v_refs[1]),
                              (w + tq, w, k_refs[2], v_refs[2])):
        kbuf[r0:r0 + nrows] = kr[...]
        v = vr[...]
        lo = lax.broadcasted_iota(jnp.int32, v.shape, 1) % LANES < HEAD_DIM
        one = jnp.ones_like(v)
        vlo_buf[r0:r0 + nrows] = jnp.where(lo, v, one)
        vhi_buf[r0:r0 + nrows] = jnp.where(lo, one, v)


def _pairs_attention(qs, k_augs, v_los, v_his, onehot2, lo_q, sinks=None):
    bq = qs[0].shape[0]
    zero = jnp.zeros_like(qs[0])
    scores = []
    for q, k_aug in zip(qs, k_augs):
        q2 = jnp.concatenate([jnp.where(lo_q, q, zero), jnp.where(lo_q, zero, q)], axis=0)
        lhs = jnp.concatenate([q2, onehot2], axis=1)
        scores.append(lax.dot_general(lhs, k_aug, (((1,), (1,)), ((), ())), preferred_element_type=F32))
    maxes = [jnp.max(s, axis=-1, keepdims=True) for s in scores]
    if sinks is not None:
        upper = lax.broadcasted_iota(jnp.int32, (2 * bq, 1), 0) >= bq
        sink_cols = [jnp.where(upper, so, se) for se, so in sinks]
        maxes = [jnp.maximum(m, sc) for m, sc in zip(maxes, sink_cols)]
    probs = [jnp.exp2(s - m).astype(BF16) for s, m in zip(scores, maxes)]
    res_e = [jnp.dot(p[:bq], v, preferred_element_type=F32) for p, v in zip(probs, v_los)]
    res_o = [jnp.dot(p[bq:], v, preferred_element_type=F32) for p, v in zip(probs, v_his)]
    outs = []
    for i, (re, ro, m) in enumerate(zip(res_e, res_o, maxes)):
        acc = jnp.where(lo_q, re, ro)
        l = pltpu.roll(jnp.where(lo_q, ro, re), HEAD_DIM, 1)
        m_rep = jnp.where(lo_q, m[:bq], m[bq:])
        if sinks is not None:
            l = l + jnp.exp2(jnp.where(lo_q, sinks[i][0], sinks[i][1]) - m_rep)
        outs.append((acc / l, m_rep + jnp.log2(l)))
    return outs


def _block_setup(w, n, n_tiles, j, nblk):
    first = (n == 0) & (j == 0)
    last = (n == n_tiles - 1) & (j == nblk - 1)
    return jnp.where(first, 1, 0) + jnp.where(last, 2, 0)


def _dilated_attn_kernel(q_ref, kl_ref, km_ref, kr_ref, vl_ref, vm_ref, vr_ref, bias_ref,
                         o_ref, lse_ref, kbuf, vlo_buf, vhi_buf, *, w, n_tiles):
    tq = q_ref.shape[0]
    nblk = tq // Q_BLOCK
    nk = Q_BLOCK + 2 * w
    n = pl.program_id(2)
    _fill_windows(kbuf, vlo_buf, vhi_buf, (kl_ref, km_ref, kr_ref), (vl_ref, vm_ref, vr_ref), w)
    lane = lax.broadcasted_iota(jnp.int32, (Q_BLOCK, LANES), 1)
    lo_q = lane < HEAD_DIM
    eye = (lax.broadcasted_iota(jnp.int32, (Q_BLOCK, Q_BLOCK), 0)
           == lax.broadcasted_iota(jnp.int32, (Q_BLOCK, Q_BLOCK), 1)).astype(BF16)
    onehot2 = jnp.concatenate([eye, eye], axis=0)

    def block(j, carry):
        r0 = pl.multiple_of(j * Q_BLOCK, Q_BLOCK)
        bias_t = bias_ref[_block_setup(w, n, n_tiles, j, nblk)]
        blocks = _lane_blocks(N_HEADS_A // 2)
        outs = _pairs_attention(
            [q_ref[pl.ds(r0, Q_BLOCK), cs] for cs in blocks],
            [jnp.concatenate([kbuf[pl.ds(r0, nk), cs], bias_t], axis=1) for cs in blocks],
            [vlo_buf[pl.ds(r0, nk), cs] for cs in blocks],
            [vhi_buf[pl.ds(r0, nk), cs] for cs in blocks], onehot2, lo_q)
        lse_tile = jnp.zeros((Q_BLOCK, LANES), F32)
        for p, (cs, (o, lse)) in enumerate(zip(blocks, outs)):
            o_ref[pl.ds(r0, Q_BLOCK), cs] = o.astype(BF16)
            keep = (lane == 2 * p) | (lane == HEAD_DIM + 2 * p + 1)
            lse_tile = lse_tile + jnp.where(keep, lse, 0.0)
        lse_ref[pl.ds(r0, Q_BLOCK), :] = lse_tile
        return carry

    lax.fori_loop(0, nblk, block, 0, unroll=True)


def _window_specs(tq, w, cols, n_halo_blocks, place):
    per = tq // w
    left = pl.BlockSpec((None, w, cols), lambda *g: place(g, jnp.maximum(g[-1] * per - 1, 0)))
    mid = pl.BlockSpec((None, tq, cols), lambda *g: place(g, g[-1]))
    right = pl.BlockSpec((None, w, cols),
                         lambda *g: place(g, jnp.minimum((g[-1] + 1) * per, n_halo_blocks - 1)))
    return [left, mid, right]


def _dilated_attn(qkv, dilation, half_window, bias_t):
    b, ln, _ = qkv.shape
    w = half_window
    tq = _pick_tile(ln, 512)
    n_tiles = ln // tq
    per_tok = 3
    col = lambda which: (lambda g, row_block: (g[0], row_block, g[1] * per_tok + which))
    q_spec = pl.BlockSpec((None, tq, QA), lambda i, r, n: (i, n, r * per_tok))
    k_specs = _window_specs(tq, w, QA, ln // w, col(1))
    v_specs = _window_specs(tq, w, QA, ln // w, col(2))
    return pl.pallas_call(
        functools.partial(_dilated_attn_kernel, w=w, n_tiles=n_tiles),
        grid=(b, dilation, n_tiles),
        in_specs=[q_spec] + k_specs + v_specs + [_const_spec(bias_t.shape)],
        out_specs=[pl.BlockSpec((None, tq, QA), lambda i, r, n: (i, n, r)),
                   pl.BlockSpec((None, tq, LANES), lambda i, r, n: (i, n, r))],
        out_shape=[jax.ShapeDtypeStruct((b, ln, dilation * QA), BF16),
                   jax.ShapeDtypeStruct((b, ln, dilation * LANES), F32)],
        scratch_shapes=[pltpu.VMEM((tq + 2 * w, QA), BF16)] * 3,
        compiler_params=pltpu.CompilerParams(
            dimension_semantics=("parallel", "parallel", "parallel"),
            vmem_limit_bytes=VMEM_LIMIT_BYTES),
        name=f"dilated_attn_d{dilation}",
    )(qkv, qkv, qkv, qkv, qkv, qkv, qkv, bias_t)


def _swa_kernel(sink_ref, q_ref, kl_ref, km_ref, kr_ref, vl_ref, vm_ref, vr_ref, bias_ref,
                o_ref, kbuf, vlo_buf, vhi_buf, *, w, n_tiles):
    tq = q_ref.shape[0]
    nblk = tq // Q_BLOCK
    nk = Q_BLOCK + 2 * w
    n = pl.program_id(1)
    _fill_windows(kbuf, vlo_buf, vhi_buf, (kl_ref, km_ref, kr_ref), (vl_ref, vm_ref, vr_ref), w)
    lo_q = lax.broadcasted_iota(jnp.int32, (Q_BLOCK, LANES), 1) < HEAD_DIM
    eye = (lax.broadcasted_iota(jnp.int32, (Q_BLOCK, Q_BLOCK), 0)
           == lax.broadcasted_iota(jnp.int32, (Q_BLOCK, Q_BLOCK), 1)).astype(BF16)
    onehot2 = jnp.concatenate([eye, eye], axis=0)
    group = N_HEADS_B // N_KV_B

    def block(j, carry):
        r0 = pl.multiple_of(j * Q_BLOCK, Q_BLOCK)
        bias_t = bias_ref[_block_setup(w, n, n_tiles, j, nblk)]
        k_aug = jnp.concatenate([kbuf[pl.ds(r0, nk), :], bias_t], axis=1)
        v_lo = vlo_buf[pl.ds(r0, nk), :]
        v_hi = vhi_buf[pl.ds(r0, nk), :]
        blocks = _lane_blocks(group)
        sinks = [(sink_ref[p] * LOG2_E, sink_ref[group + p] * LOG2_E) for p in range(group)]
        outs = _pairs_attention([q_ref[pl.ds(r0, Q_BLOCK), cs] for cs in blocks], [k_aug] * group,
                                [v_lo] * group, [v_hi] * group, onehot2, lo_q, sinks)
        for cs, (o, _) in zip(blocks, outs):
            o_ref[pl.ds(r0, Q_BLOCK), cs] = o.astype(BF16)
        return carry

    lax.fori_loop(0, nblk, block, 0, unroll=True)


def _swa_attn(q, k, v, sink, half_window, bias_t):
    b, s, cols = q.shape
    w = half_window
    tq = _pick_tile(s, 512)
    n_tiles = s // tq
    kv_specs = _window_specs(tq, w, KB, s // w, lambda g, row_block: (g[0], row_block, 0))
    return pl.pallas_call(
        functools.partial(_swa_kernel, w=w, n_tiles=n_tiles),
        grid=(b, n_tiles),
        in_specs=[pl.BlockSpec(memory_space=pltpu.SMEM),
                  pl.BlockSpec((None, tq, cols), lambda i, n: (i, n, 0))] + kv_specs + kv_specs
                 + [_const_spec(bias_t.shape)],
        out_specs=pl.BlockSpec((None, tq, cols), lambda i, n: (i, n, 0)),
        out_shape=jax.ShapeDtypeStruct((b, s, cols), BF16),
        scratch_shapes=[pltpu.VMEM((tq + 2 * w, KB), BF16)] * 3,
        compiler_params=pltpu.CompilerParams(
            dimension_semantics=("parallel", "parallel"), vmem_limit_bytes=VMEM_LIMIT_BYTES),
        name="swa_attn",
    )(sink, q, k, k, k, v, v, v, bias_t)


def _out_ffn_kernel(x_ref, o1_ref, o4_ref, o16_ref, l1_ref, l4_ref, l16_ref, ob_ref,
                    expand_ref, wo_ref, g_ref, wg_ref, wu_ref, wd_ref, y_ref,
                    acc_ref, lnat_ref, onat_ref, *, ff_chunk):
    tm = x_ref.shape[0]
    for slot, (d, l_ref, o_ref) in enumerate(((4, l4_ref, o4_ref), (16, l16_ref, o16_ref))):
        for res in range(d):
            rows = pl.ds(res, tm // d, stride=d)
            lnat_ref[slot, rows, :] = l_ref[:, res * LANES:(res + 1) * LANES]
            for blk in range(QA // LANES):
                c0 = res * QA + blk * LANES
                onat_ref[slot, blk, rows, :] = o_ref[:, c0:c0 + LANES].astype(F32)

    l1, l2, l3 = l1_ref[...], lnat_ref[0], lnat_ref[1]
    mx = jnp.maximum(jnp.maximum(l1, l2), l3)
    e1, e2, e3 = jnp.exp2(l1 - mx), jnp.exp2(l2 - mx), jnp.exp2(l3 - mx)
    inv = 1.0 / (e1 + e2 + e3)

    def expand(wt):
        hi = wt.astype(BF16)
        lo = (wt - hi.astype(F32)).astype(BF16)
        return (jnp.dot(hi, expand_ref[...], preferred_element_type=F32)
                + jnp.dot(lo, expand_ref[...], preferred_element_type=F32))

    o2 = jnp.concatenate([onat_ref[0, blk] for blk in range(QA // LANES)], axis=1)
    o3 = jnp.concatenate([onat_ref[1, blk] for blk in range(QA // LANES)], axis=1)
    out_a = (expand(e1 * inv) * o1_ref[...].astype(F32) + expand(e2 * inv) * o2
             + expand(e3 * inv) * o3).astype(BF16)
    x1 = (x_ref[...]
          + jnp.dot(out_a, wo_ref[0:QA, :], preferred_element_type=F32)
          + jnp.dot(ob_ref[...], wo_ref[QA:QA + QB, :], preferred_element_type=F32))
    ms = jnp.mean(x1 * x1, axis=-1, keepdims=True)
    h2 = (x1 * lax.rsqrt(ms + EPS) * g_ref[...]).astype(BF16)
    acc_ref[...] = x1

    def ff_chunk_step(ch, carry):
        c0 = pl.multiple_of(ch * ff_chunk, ff_chunk)
        gate = jnp.dot(h2, wg_ref[:, pl.ds(c0, ff_chunk)], preferred_element_type=F32)
        up = jnp.dot(h2, wu_ref[:, pl.ds(c0, ff_chunk)], preferred_element_type=F32)
        act = (gate / (1.0 + jnp.exp(-gate)) * up).astype(BF16)
        acc_ref[...] += jnp.dot(act, wd_ref[pl.ds(c0, ff_chunk), :], preferred_element_type=F32)
        return carry

    lax.fori_loop(0, wg_ref.shape[1] // ff_chunk, ff_chunk_step, 0)
    y_ref[...] = acc_ref[...]


def _out_ffn(x, o_branches, lse_branches, ob, expand, wo, ffn_norm, wg, wu, wd):
    b, s, d = x.shape
    tm = _pick_tile(s, 512)
    tok = lambda rows, cols: pl.BlockSpec((None, rows, cols), lambda i, n: (i, n, 0))
    dils = [dl for _, dl in DILATED_PATTERNS]
    consts = (expand, wo, ffn_norm.reshape(1, d), wg, wu, wd)
    return pl.pallas_call(
        functools.partial(_out_ffn_kernel, ff_chunk=MXU_COLS),
        grid=(b, s // tm),
        in_specs=([tok(tm, d)] + [tok(tm // dl, dl * QA) for dl in dils]
                  + [tok(tm // dl, dl * LANES) for dl in dils] + [tok(tm, QB)]
                  + [_const_spec(t.shape) for t in consts]),
        out_specs=tok(tm, d),
        out_shape=jax.ShapeDtypeStruct((b, s, d), F32),
        scratch_shapes=[pltpu.VMEM((tm, d), F32), pltpu.VMEM((2, tm, LANES), F32),
                        pltpu.VMEM((2, QA // LANES, tm, LANES), F32)],
        compiler_params=pltpu.CompilerParams(
            dimension_semantics=("parallel", "parallel"), vmem_limit_bytes=VMEM_LIMIT_BYTES),
        name="out_ffn",
    )(x, *o_branches, *lse_branches, ob, *consts)


def _swa_head_order():
    group = N_HEADS_B // N_KV_B
    return [h for p in range(group) for h in (p, group + p)]


def _head_cols(heads):
    return jnp.concatenate([jnp.arange(h * HEAD_DIM, (h + 1) * HEAD_DIM) for h in heads])


def _rope_tables(s):
    half = HEAD_DIM // 2
    inv = ROPE_THETA ** (-2.0 * jnp.arange(half, dtype=F32) / HEAD_DIM)
    ang = jnp.arange(s).astype(F32)[:, None] * inv[None, :]
    cos, sin = jnp.cos(ang), jnp.sin(ang)
    reps = LANES // HEAD_DIM
    return (jnp.tile(jnp.concatenate([cos, cos], axis=-1), (1, reps)),
            jnp.tile(jnp.concatenate([-sin, sin], axis=-1), (1, reps)))


def _gain_rows(qnorm_a, knorm_a, qnorm_b, knorm_b):
    half = HEAD_DIM // 2
    scale = HEAD_DIM ** -0.5 * LOG2_E
    rows = [qnorm_a * scale, knorm_a, qnorm_b * scale, knorm_b]
    swapped = [jnp.concatenate([g[half:], g[:half]]) for g in rows]
    return jnp.stack([jnp.tile(g.astype(F32), LANES // HEAD_DIM) for g in rows + swapped])


def _lse_expand_matrix():
    lane = lax.broadcasted_iota(jnp.int32, (LANES, QA), 0)
    head = lax.broadcasted_iota(jnp.int32, (LANES, QA), 1) // HEAD_DIM
    src = jnp.where(head % 2 == 0, head, HEAD_DIM + head)
    return (lane == src).astype(BF16)


def _encoder_layer(x, tables, prm):
    cos_t, sin_t = tables
    a1, a4, a16, qb, kb, vb = _in_proj(x, prm["attn_norm"], prm["w_in"], cos_t, sin_t, prm["gains"])
    branches = [_dilated_attn(qkv, dilation, (window // 2) // dilation, prm["bias_a"])
                for qkv, (window, dilation) in zip((a1, a4, a16), DILATED_PATTERNS)]
    ob = _swa_attn(qb, kb, vb, prm["sink"], SWA_HALF_WINDOW, prm["bias_b"])
    return _out_ffn(x, [o for o, _ in branches], [l for _, l in branches], ob,
                    prm["expand"], prm["w_out"], prm["ffn_norm"], prm["w_gate"], prm["w_up"], prm["w_down"])


def kernel(x_prompt, x_sample, attn_norm, w_in, qnorm_a, knorm_a, qnorm_b, knorm_b, sink_b, w_out,
           ffn_norm, w_gate, w_up, w_down):
    depth = w_in.shape[0]
    order = _swa_head_order()
    qb0 = 3 * QA
    in_cols = jnp.concatenate([jnp.arange(qb0), qb0 + _head_cols(order),
                               jnp.arange(qb0 + QB, w_in.shape[-1])])
    out_rows = jnp.concatenate([jnp.arange(QA), QA + _head_cols(order)])
    half_windows = {(window // 2) // dilation for window, dilation in DILATED_PATTERNS}
    assert len(half_windows) == 1
    bias_a = _band_bias_t(half_windows.pop())
    bias_b = _band_bias_t(SWA_HALF_WINDOW)
    expand = _lse_expand_matrix()
    tables = {x.shape[1]: _rope_tables(x.shape[1]) for x in (x_prompt, x_sample)}
    ys = [x_prompt, x_sample]
    for i in range(depth):
        prm = dict(
            attn_norm=attn_norm[i], ffn_norm=ffn_norm[i],
            w_in=w_in[i][:, in_cols].astype(BF16),
            w_out=w_out[i][out_rows, :].astype(BF16),
            w_gate=w_gate[i].astype(BF16), w_up=w_up[i].astype(BF16), w_down=w_down[i].astype(BF16),
            gains=_gain_rows(qnorm_a[i], knorm_a[i], qnorm_b[i], knorm_b[i]),
            sink=sink_b[i].astype(F32), expand=expand, bias_a=bias_a, bias_b=bias_b)
        ys = [_encoder_layer(y, tables[y.shape[1]], prm) for y in ys]
    return tuple(ys)
```

```python
import functools
import math

import jax
import jax.numpy as jnp
from jax import lax
from jax.experimental import pallas as pl
from jax.experimental.pallas import tpu as pltpu

HEAD_DIM = 64
N_HEADS_A = 8
N_HEADS_B = 8
N_KV_B = 2
DILATED_PATTERNS = ((128, 1), (512, 4), (2048, 16))
SWA_HALF_WINDOW = 128
ROPE_THETA = 10000.0
EPS = 1e-6

QA = N_HEADS_A * HEAD_DIM
QB = N_HEADS_B * HEAD_DIM
KB = N_KV_B * HEAD_DIM
QKV_A = 3 * QA
LANES = 128
MXU_COLS = 256
N_MXU = 2
Q_BLOCK = 128
TOKEN_TILE = 512
ATTN_TILE = 1024
NEG_BIG = -1e30
VMEM_LIMIT_BYTES = 56 * 1024 * 1024
LOG2_E = math.log2(math.e)

BF16 = jnp.bfloat16
F32 = jnp.float32


def _pick_tile(n, cap):
    t = cap
    while n % t:
        t //= 2
    return t


def _const_spec(shape):
    return pl.BlockSpec(shape, lambda *_: (0,) * len(shape), pipeline_mode=pl.Buffered(1))


def _lane_blocks(n, start=0):
    return [slice(start + i * LANES, start + (i + 1) * LANES) for i in range(n)]


def _software_pipeline(n_items, stages):
    states = [None] * n_items
    for t in range(n_items + len(stages) - 1):
        for s, stage in enumerate(stages):
            if 0 <= t - s < n_items:
                states[t - s] = stage(t - s, states[t - s])


def _in_proj_kernel(x_ref, g_ref, w_ref, cos_ref, sin_ref, gains_ref,
                    a1_ref, a4_ref, a16_ref, qb_ref, kb_ref, vb_ref, stage1_ref, stage4_ref):
    tm = x_ref.shape[0]
    x = x_ref[...]
    ms = jnp.mean(x * x, axis=-1, keepdims=True)
    h = (x * lax.rsqrt(ms + EPS) * g_ref[...]).astype(BF16)

    r = lax.broadcasted_iota(jnp.int32, (LANES, LANES), 0)
    c = lax.broadcasted_iota(jnp.int32, (LANES, LANES), 1)
    ones_blk = ((r // HEAD_DIM) == (c // HEAD_DIM)).astype(BF16)
    lane = lax.broadcasted_iota(jnp.int32, (tm, LANES), 1)
    first_half = (lane % HEAD_DIM) < (HEAD_DIM // 2)

    cos = cos_ref[...]
    sin = sin_ref[...]
    rope_a = [gains_ref[k:k + 1, :] * cos for k in range(4)]
    rope_b = [gains_ref[4 + k:5 + k, :] * sin for k in range(4)]

    def norm_rope(p, kind):
        ss = jnp.dot((p * p).astype(BF16), ones_blk, preferred_element_type=F32)
        rinv = lax.rsqrt(ss * (1.0 / HEAD_DIM) + EPS)
        swapped = jnp.where(first_half,
                            pltpu.roll(p, LANES - HEAD_DIM // 2, 1),
                            pltpu.roll(p, HEAD_DIM // 2, 1))
        return rinv * (p * rope_a[kind] + swapped * rope_b[kind])

    n_a = QKV_A // LANES
    kinds = [0] * 4 + [1] * 4 + [None] * 4 + [2] * 4 + [3, None]
    b_outs = [(qb_ref, i) for i in range(4)] + [(kb_ref, 0), (vb_ref, 0)]

    def epilogue(blk, p):
        val = p if kinds[blk] is None else norm_rope(p, kinds[blk])
        if blk >= n_a:
            out_ref, i = b_outs[blk - n_a]
            out_ref[:, i * LANES:(i + 1) * LANES] = val.astype(BF16)
            return
        a1_ref[:, blk * LANES:(blk + 1) * LANES] = val.astype(BF16)
        stage1_ref[blk] = val
        for r4 in range(4):
            rows4 = stage1_ref[blk, pl.ds(r4, tm // 4, stride=4), :]
            a4_ref[:, r4 * QKV_A + blk * LANES:r4 * QKV_A + (blk + 1) * LANES] = rows4.astype(BF16)
            stage4_ref[blk, r4] = rows4
        for r4 in range(4):
            for j in range(4):
                rows16 = stage4_ref[blk, r4, pl.ds(j, tm // 16, stride=4), :]
                c0 = (r4 + 4 * j) * QKV_A + blk * LANES
                a16_ref[:, c0:c0 + LANES] = rows16.astype(BF16)

    chunk = N_MXU * MXU_COLS
    bounds = [(c0, min(c0 + chunk, w_ref.shape[1])) for c0 in range(0, w_ref.shape[1], chunk)]

    def project(i, _):
        c0, c1 = bounds[i]
        return jnp.dot(h, w_ref[:, c0:c1], preferred_element_type=F32)

    def finish(i, proj):
        c0, c1 = bounds[i]
        for k in range((c1 - c0) // LANES):
            epilogue(c0 // LANES + k, proj[:, k * LANES:(k + 1) * LANES])

    _software_pipeline(len(bounds), [project, finish])


def _in_proj(x, attn_norm, w_in_bf16, cos_t, sin_t, gains):
    b, s, d = x.shape
    tm = _pick_tile(s, TOKEN_TILE)
    tok = lambda rows, cols: pl.BlockSpec((None, rows, cols), lambda i, n: (i, n, 0))
    tab = pl.BlockSpec((tm, LANES), lambda i, n: (n, 0))
    outs = [(tm // dl, s // dl, dl * QKV_A) for dl in (1, 4, 16)] + [(tm, s, QB), (tm, s, KB), (tm, s, KB)]
    return pl.pallas_call(
        _in_proj_kernel,
        grid=(b, s // tm),
        in_specs=[tok(tm, d), _const_spec((1, d)), _const_spec(w_in_bf16.shape), tab, tab,
                  _const_spec(gains.shape)],
        out_specs=[tok(rows, cols) for rows, _, cols in outs],
        out_shape=[jax.ShapeDtypeStruct((b, ln, cols), BF16) for _, ln, cols in outs],
        scratch_shapes=[pltpu.VMEM((QKV_A // LANES, tm, LANES), F32),
                        pltpu.VMEM((QKV_A // LANES, 4, tm // 4, LANES), F32)],
        compiler_params=pltpu.CompilerParams(
            dimension_semantics=("parallel", "parallel"), vmem_limit_bytes=VMEM_LIMIT_BYTES),
        name="in_proj",
    )(x, attn_norm.reshape(1, d), w_in_bf16, cos_t, sin_t, gains)


def _band_bias_t(w):
    nk = Q_BLOCK + 2 * w
    cc = jnp.arange(nk)[:, None]
    aa = jnp.arange(Q_BLOCK)[None, :]
    band = (cc >= aa) & (cc <= aa + 2 * w)
    variants = [band & ((cc >= w) | (e & 1 == 0)) & ((cc < nk - w) | (e & 2 == 0)) for e in range(4)]
    return jnp.where(jnp.stack(variants), 0.0, NEG_BIG).astype(BF16)


def _window_rows(left_ref, mid_ref, right_ref, j, w, cols):
    tq = mid_ref.shape[0]
    lo, hi = j * Q_BLOCK - w, (j + 1) * Q_BLOCK + w
    parts = []
    if lo < 0:
        parts.append(left_ref[w + lo:w, cols])
    parts.append(mid_ref[max(lo, 0):min(hi, tq), cols])
    if hi > tq:
        parts.append(right_ref[0:hi - tq, cols])
    return parts[0] if len(parts) == 1 else jnp.concatenate(parts, axis=0)


def _mask_variant(n, n_tiles, j, nblk):
    first = (n == 0) & (j == 0)
    last = (n == n_tiles - 1) & (j == nblk - 1)
    return jnp.where(first, 1, 0) + jnp.where(last, 2, 0)


def _attention_items(items, sinks=None):
    bq = Q_BLOCK
    lo_q = lax.broadcasted_iota(jnp.int32, (bq, LANES), 1) < HEAD_DIM
    eye = (lax.broadcasted_iota(jnp.int32, (bq, bq), 0)
           == lax.broadcasted_iota(jnp.int32, (bq, bq), 1)).astype(BF16)
    onehot2 = jnp.concatenate([eye, eye], axis=0)

    def scores(i, _):
        bias_t = items[i]["bias"]()
        out = []
        for q, k in zip(items[i]["q"](), items[i]["k"]()):
            zero = jnp.zeros_like(q)
            q2 = jnp.concatenate([jnp.where(lo_q, q, zero), jnp.where(lo_q, zero, q)], axis=0)
            lhs = jnp.concatenate([q2, onehot2], axis=1)
            k_aug = jnp.concatenate([k, bias_t], axis=1)
            out.append(lax.dot_general(lhs, k_aug, (((1,), (1,)), ((), ())), preferred_element_type=F32))
        return out

    def softmax(i, sc):
        maxes = [jnp.max(s, axis=-1, keepdims=True) for s in sc]
        if sinks is not None:
            upper = lax.broadcasted_iota(jnp.int32, (2 * bq, 1), 0) >= bq
            maxes = [jnp.maximum(m, jnp.where(upper, so, se)) for m, (se, so) in zip(maxes, sinks)]
        return maxes, [jnp.exp2(s - m).astype(BF16) for s, m in zip(sc, maxes)]

    def weighted_values(i, st):
        maxes, probs = st
        res, augmented = [], {}
        for p, v in zip(probs, items[i]["v"]()):
            if id(v) not in augmented:
                lo_v = lax.broadcasted_iota(jnp.int32, v.shape, 1) < HEAD_DIM
                one = jnp.ones_like(v)
                augmented[id(v)] = (jnp.where(lo_v, v, one), jnp.where(lo_v, one, v))
            v_lo, v_hi = augmented[id(v)]
            res.append((jnp.dot(p[:bq], v_lo, preferred_element_type=F32),
                        jnp.dot(p[bq:], v_hi, preferred_element_type=F32)))
        return maxes, res

    def finish(i, st):
        maxes, res = st
        outs = []
        for k, ((re, ro), m) in enumerate(zip(res, maxes)):
            acc = jnp.where(lo_q, re, ro)
            l = pltpu.roll(jnp.where(lo_q, ro, re), HEAD_DIM, 1)
            m_rep = jnp.where(lo_q, m[:bq], m[bq:])
            if sinks is not None:
                l = l + jnp.exp2(jnp.where(lo_q, sinks[k][0], sinks[k][1]) - m_rep)
            outs.append((acc / l, m_rep + jnp.log2(l)))
        items[i]["store"](outs)
        return None

    _software_pipeline(len(items), [scores, softmax, weighted_values, finish])


def _dilated_attn_kernel(mid_ref, left_ref, right_ref, bias_ref, o_ref, lse_ref, *, w, n_tiles, n_res):
    tq = mid_ref.shape[0]
    nblk = tq // Q_BLOCK
    n = pl.program_id(2)
    lane = lax.broadcasted_iota(jnp.int32, (Q_BLOCK, LANES), 1)
    n_pairs = N_HEADS_A // 2

    def item(res, j):
        rows = slice(j * Q_BLOCK, (j + 1) * Q_BLOCK)
        base = res * QKV_A

        def store(outs):
            lse_tile = jnp.zeros((Q_BLOCK, LANES), F32)
            for p, (o, lse) in enumerate(outs):
                o_ref[rows, res * QA + p * LANES:res * QA + (p + 1) * LANES] = o.astype(BF16)
                keep = (lane == 2 * p) | (lane == HEAD_DIM + 2 * p + 1)
                lse_tile = lse_tile + jnp.where(keep, lse, 0.0)
            lse_ref[rows, res * LANES:(res + 1) * LANES] = lse_tile

        window = lambda c0: [_window_rows(left_ref, mid_ref, right_ref, j, w, cs)
                             for cs in _lane_blocks(n_pairs, base + c0)]
        return dict(q=lambda: [mid_ref[rows, cs] for cs in _lane_blocks(n_pairs, base)],
                    k=lambda: window(QA), v=lambda: window(2 * QA),
                    bias=lambda: bias_ref[_mask_variant(n, n_tiles, j, nblk)], store=store)

    _attention_items([item(res, j) for res in range(n_res) for j in range(nblk)])


def _dilated_attn(qkv, dilation, half_window, bias_t):
    b, ln, _ = qkv.shape
    w = half_window
    tq = _pick_tile(ln, ATTN_TILE)
    n_res = min(dilation, ATTN_TILE // tq)
    n_tiles = ln // tq
    per = tq // w
    last_halo = ln // w - 1
    in_cols = n_res * QKV_A
    return pl.pallas_call(
        functools.partial(_dilated_attn_kernel, w=w, n_tiles=n_tiles, n_res=n_res),
        grid=(b, dilation // n_res, n_tiles),
        in_specs=[pl.BlockSpec((None, tq, in_cols), lambda i, r, n: (i, n, r)),
                  pl.BlockSpec((None, w, in_cols), lambda i, r, n: (i, jnp.maximum(n * per - 1, 0), r)),
                  pl.BlockSpec((None, w, in_cols), lambda i, r, n: (i, jnp.minimum((n + 1) * per, last_halo), r)),
                  _const_spec(bias_t.shape)],
        out_specs=[pl.BlockSpec((None, tq, n_res * QA), lambda i, r, n: (i, n, r)),
                   pl.BlockSpec((None, tq, n_res * LANES), lambda i, r, n: (i, n, r))],
        out_shape=[jax.ShapeDtypeStruct((b, ln, dilation * QA), BF16),
                   jax.ShapeDtypeStruct((b, ln, dilation * LANES), F32)],
        compiler_params=pltpu.CompilerParams(
            dimension_semantics=("parallel", "parallel", "parallel"),
            vmem_limit_bytes=VMEM_LIMIT_BYTES),
        name=f"dilated_attn_d{dilation}",
    )(qkv, qkv, qkv, bias_t)


def _swa_kernel(sink_ref, q_ref, km_ref, kl_ref, kr_ref, vm_ref, vl_ref, vr_ref, bias_ref, o_ref,
                *, w, n_tiles):
    tq = q_ref.shape[0]
    nblk = tq // Q_BLOCK
    n = pl.program_id(1)
    group = N_HEADS_B // N_KV_B
    all_lanes = slice(0, LANES)
    sinks = [(sink_ref[p] * LOG2_E, sink_ref[group + p] * LOG2_E) for p in range(group)]

    def item(j):
        rows = slice(j * Q_BLOCK, (j + 1) * Q_BLOCK)

        def store(outs):
            for cs, (o, _) in zip(_lane_blocks(group), outs):
                o_ref[rows, cs] = o.astype(BF16)

        return dict(q=lambda: [q_ref[rows, cs] for cs in _lane_blocks(group)],
                    k=lambda: [_window_rows(kl_ref, km_ref, kr_ref, j, w, all_lanes)] * group,
                    v=lambda: [_window_rows(vl_ref, vm_ref, vr_ref, j, w, all_lanes)] * group,
                    bias=lambda: bias_ref[_mask_variant(n, n_tiles, j, nblk)], store=store)

    _attention_items([item(j) for j in range(nblk)], sinks)


def _swa_attn(q, k, v, sink, half_window, bias_t):
    b, s, cols = q.shape
    w = half_window
    tq = _pick_tile(s, ATTN_TILE)
    n_tiles = s // tq
    per = tq // w
    last_halo = s // w - 1
    kv_specs = [pl.BlockSpec((None, tq, KB), lambda i, n: (i, n, 0)),
                pl.BlockSpec((None, w, KB), lambda i, n: (i, jnp.maximum(n * per - 1, 0), 0)),
                pl.BlockSpec((None, w, KB), lambda i, n: (i, jnp.minimum((n + 1) * per, last_halo), 0))]
    return pl.pallas_call(
        functools.partial(_swa_kernel, w=w, n_tiles=n_tiles),
        grid=(b, n_tiles),
        in_specs=[pl.BlockSpec(memory_space=pltpu.SMEM),
                  pl.BlockSpec((None, tq, cols), lambda i, n: (i, n, 0))] + kv_specs + kv_specs
                 + [_const_spec(bias_t.shape)],
        out_specs=pl.BlockSpec((None, tq, cols), lambda i, n: (i, n, 0)),
        out_shape=jax.ShapeDtypeStruct((b, s, cols), BF16),
        compiler_params=pltpu.CompilerParams(
            dimension_semantics=("parallel", "parallel"), vmem_limit_bytes=VMEM_LIMIT_BYTES),
        name="swa_attn",
    )(sink, q, k, k, k, v, v, v, bias_t)


def _out_ffn_kernel(x_ref, o1_ref, o4_ref, o16_ref, l1_ref, l4_ref, l16_ref, ob_ref,
                    expand_ref, wo_ref, g_ref, wg_ref, wu_ref, wd_ref, y_ref,
                    x1_ref, act_ref, lnat_ref, onat_ref, *, ff_chunk):
    tm = x_ref.shape[0]
    for slot, (d, l_ref, o_ref) in enumerate(((4, l4_ref, o4_ref), (16, l16_ref, o16_ref))):
        for res in range(d):
            rows = pl.ds(res, tm // d, stride=d)
            lnat_ref[slot, rows, :] = l_ref[:, res * LANES:(res + 1) * LANES]
            for blk in range(QA // LANES):
                c0 = res * QA + blk * LANES
                onat_ref[slot, blk, rows, :] = o_ref[:, c0:c0 + LANES].astype(F32)

    l1, l2, l3 = l1_ref[...], lnat_ref[0], lnat_ref[1]
    mx = jnp.maximum(jnp.maximum(l1, l2), l3)
    e1, e2, e3 = jnp.exp2(l1 - mx), jnp.exp2(l2 - mx), jnp.exp2(l3 - mx)
    inv = 1.0 / (e1 + e2 + e3)

    def expand(wt):
        return jnp.dot(wt.astype(BF16), expand_ref[...], preferred_element_type=F32)

    o2 = jnp.concatenate([onat_ref[0, blk] for blk in range(QA // LANES)], axis=1)
    o3 = jnp.concatenate([onat_ref[1, blk] for blk in range(QA // LANES)], axis=1)
    out_a = (expand(e1 * inv) * o1_ref[...].astype(F32) + expand(e2 * inv) * o2
             + expand(e3 * inv) * o3).astype(BF16)
    x1 = (x_ref[...]
          + jnp.dot(out_a, wo_ref[0:QA, :], preferred_element_type=F32)
          + jnp.dot(ob_ref[...], wo_ref[QA:QA + QB, :], preferred_element_type=F32))
    ms = jnp.mean(x1 * x1, axis=-1, keepdims=True)
    h2 = (x1 * lax.rsqrt(ms + EPS) * g_ref[...]).astype(BF16)
    x1_ref[...] = x1

    def gate_up(i, _):
        cs = slice(i * ff_chunk, (i + 1) * ff_chunk)
        return (jnp.dot(h2, wg_ref[:, cs], preferred_element_type=F32),
                jnp.dot(h2, wu_ref[:, cs], preferred_element_type=F32))

    def swiglu(i, gu):
        gate, up = gu
        act_ref[:, i * ff_chunk:(i + 1) * ff_chunk] = (gate / (1.0 + jnp.exp(-gate)) * up).astype(BF16)

    _software_pipeline(wg_ref.shape[1] // ff_chunk, [gate_up, swiglu])
    y_ref[...] = x1_ref[...] + jnp.dot(act_ref[...], wd_ref[...], preferred_element_type=F32)


def _out_ffn(x, o_branches, lse_branches, ob, expand, wo, ffn_norm, wg, wu, wd):
    b, s, d = x.shape
    tm = _pick_tile(s, TOKEN_TILE)
    tok = lambda rows, cols: pl.BlockSpec((None, rows, cols), lambda i, n: (i, n, 0))
    dils = [dl for _, dl in DILATED_PATTERNS]
    consts = (expand, wo, ffn_norm.reshape(1, d), wg, wu, wd)
    return pl.pallas_call(
        functools.partial(_out_ffn_kernel, ff_chunk=MXU_COLS),
        grid=(b, s // tm),
        in_specs=([tok(tm, d)] + [tok(tm // dl, dl * QA) for dl in dils]
                  + [tok(tm // dl, dl * LANES) for dl in dils] + [tok(tm, QB)]
                  + [_const_spec(t.shape) for t in consts]),
        out_specs=tok(tm, d),
        out_shape=jax.ShapeDtypeStruct((b, s, d), F32),
        scratch_shapes=[pltpu.VMEM((tm, d), F32), pltpu.VMEM((tm, wg.shape[1]), BF16),
                        pltpu.VMEM((2, tm, LANES), F32),
                        pltpu.VMEM((2, QA // LANES, tm, LANES), F32)],
        compiler_params=pltpu.CompilerParams(
            dimension_semantics=("parallel", "parallel"), vmem_limit_bytes=VMEM_LIMIT_BYTES),
        name="out_ffn",
    )(x, *o_branches, *lse_branches, ob, *consts)


def _swa_head_order():
    group = N_HEADS_B // N_KV_B
    return [h for p in range(group) for h in (p, group + p)]


def _head_cols(heads):
    return jnp.concatenate([jnp.arange(h * HEAD_DIM, (h + 1) * HEAD_DIM) for h in heads])


def _rope_tables(s):
    half = HEAD_DIM // 2
    inv = ROPE_THETA ** (-2.0 * jnp.arange(half, dtype=F32) / HEAD_DIM)
    ang = jnp.arange(s).astype(F32)[:, None] * inv[None, :]
    cos, sin = jnp.cos(ang), jnp.sin(ang)
    reps = LANES // HEAD_DIM
    return (jnp.tile(jnp.concatenate([cos, cos], axis=-1), (1, reps)),
            jnp.tile(jnp.concatenate([-sin, sin], axis=-1), (1, reps)))


def _gain_rows(qnorm_a, knorm_a, qnorm_b, knorm_b):
    half = HEAD_DIM // 2
    scale = HEAD_DIM ** -0.5 * LOG2_E
    rows = [qnorm_a * scale, knorm_a, qnorm_b * scale, knorm_b]
    swapped = [jnp.concatenate([g[half:], g[:half]]) for g in rows]
    return jnp.stack([jnp.tile(g.astype(F32), LANES // HEAD_DIM) for g in rows + swapped])


def _lse_expand_matrix():
    lane = lax.broadcasted_iota(jnp.int32, (LANES, QA), 0)
    head = lax.broadcasted_iota(jnp.int32, (LANES, QA), 1) // HEAD_DIM
    src = jnp.where(head % 2 == 0, head, HEAD_DIM + head)
    return (lane == src).astype(BF16)


def _encoder_layer(x, tables, prm):
    cos_t, sin_t = tables
    a1, a4, a16, qb, kb, vb = _in_proj(x, prm["attn_norm"], prm["w_in"], cos_t, sin_t, prm["gains"])
    branches = [_dilated_attn(qkv, dilation, (window // 2) // dilation, prm["bias_a"])
                for qkv, (window, dilation) in zip((a1, a4, a16), DILATED_PATTERNS)]
    ob = _swa_attn(qb, kb, vb, prm["sink"], SWA_HALF_WINDOW, prm["bias_b"])
    return _out_ffn(x, [o for o, _ in branches], [l for _, l in branches], ob,
                    prm["expand"], prm["w_out"], prm["ffn_norm"], prm["w_gate"], prm["w_up"], prm["w_down"])


def kernel(x_prompt, x_sample, attn_norm, w_in, qnorm_a, knorm_a, qnorm_b, knorm_b, sink_b, w_out,
           ffn_norm, w_gate, w_up, w_down):
    depth = w_in.shape[0]
    order = _swa_head_order()
    in_cols = jnp.concatenate([jnp.arange(QKV_A), QKV_A + _head_cols(order),
                               jnp.arange(QKV_A + QB, w_in.shape[-1])])
    out_rows = jnp.concatenate([jnp.arange(QA), QA + _head_cols(order)])
    half_windows = {(window // 2) // dilation for window, dilation in DILATED_PATTERNS}
    assert len(half_windows) == 1
    bias_a = _band_bias_t(half_windows.pop())
    bias_b = _band_bias_t(SWA_HALF_WINDOW)
    expand = _lse_expand_matrix()
    tables = {x.shape[1]: _rope_tables(x.shape[1]) for x in (x_prompt, x_sample)}
    ys = [x_prompt, x_sample]
    for i in range(depth):
        prm = dict(
            attn_norm=attn_norm[i], ffn_norm=ffn_norm[i],
            w_in=w_in[i][:, in_cols].astype(BF16),
            w_out=w_out[i][out_rows, :].astype(BF16),
            w_gate=w_gate[i].astype(BF16), w_up=w_up[i].astype(BF16), w_down=w_down[i].astype(BF16),
            gains=_gain_rows(qnorm_a[i], knorm_a[i], qnorm_b[i], knorm_b[i]),
            sink=sink_b[i].astype(F32), expand=expand, bias_a=bias_a, bias_b=bias_b)
        ys = [_encoder_layer(y, tables[y.shape[1]], prm) for y in ys]
    return tuple(ys)
```

```python
import functools
import math

import jax
import jax.numpy as jnp
from jax import lax
from jax.experimental import pallas as pl
from jax.experimental.pallas import tpu as pltpu

HEAD_DIM = 64
N_HEADS_A = 8
N_HEADS_B = 8
N_KV_B = 2
DILATED_PATTERNS = ((128, 1), (512, 4), (2048, 16))
SWA_HALF_WINDOW = 128
ROPE_THETA = 10000.0
EPS = 1e-6

QA = N_HEADS_A * HEAD_DIM
QB = N_HEADS_B * HEAD_DIM
KB = N_KV_B * HEAD_DIM
QKV_A = 3 * QA
LANES = 128
MXU_COLS = 256
N_MXU = 2
Q_BLOCK = 128
TOKEN_TILE = 512
ATTN_TILE = 1024
NEG_BIG = -1e30
VMEM_LIMIT_BYTES = 56 * 1024 * 1024
LOG2_E = math.log2(math.e)

BF16 = jnp.bfloat16
F32 = jnp.float32


def _pick_tile(n, cap):
    t = cap
    while n % t:
        t //= 2
    return t


def _const_spec(shape):
    return pl.BlockSpec(shape, lambda *_: (0,) * len(shape), pipeline_mode=pl.Buffered(1))


def _lane_blocks(n, start=0):
    return [slice(start + i * LANES, start + (i + 1) * LANES) for i in range(n)]


def _software_pipeline(n_items, stages):
    states = [None] * n_items
    for t in range(n_items + len(stages) - 1):
        for s, stage in enumerate(stages):
            if 0 <= t - s < n_items:
                states[t - s] = stage(t - s, states[t - s])


def _in_proj_kernel(x_ref, g_ref, w_ref, cos_ref, sin_ref, gains_ref,
                    a1_ref, a4_ref, a16_ref, qb_ref, kb_ref, vb_ref, stage1_ref, stage4_ref):
    tm = x_ref.shape[0]
    x = x_ref[...]
    ms = jnp.mean(x * x, axis=-1, keepdims=True)
    h = (x * lax.rsqrt(ms + EPS) * g_ref[...]).astype(BF16)

    r = lax.broadcasted_iota(jnp.int32, (MXU_COLS, MXU_COLS), 0)
    c = lax.broadcasted_iota(jnp.int32, (MXU_COLS, MXU_COLS), 1)
    ones_blk = ((r // HEAD_DIM) == (c // HEAD_DIM)).astype(BF16)
    lane = lax.broadcasted_iota(jnp.int32, (tm, LANES), 1)
    first_half = (lane % HEAD_DIM) < (HEAD_DIM // 2)

    cos = cos_ref[...]
    sin = sin_ref[...]
    rope_a = [gains_ref[k:k + 1, :] * cos for k in range(4)]
    rope_b = [gains_ref[4 + k:5 + k, :] * sin for k in range(4)]

    def sum_squares(ps):
        sq = jnp.concatenate([(p * p).astype(BF16) for p in ps], axis=1)
        width = sq.shape[1]
        ss = jnp.dot(sq, ones_blk[:width, :width], preferred_element_type=F32)
        return [ss[:, k * LANES:(k + 1) * LANES] for k in range(len(ps))]

    def norm_rope(p, ss, kind):
        rinv = lax.rsqrt(ss * (1.0 / HEAD_DIM) + EPS)
        swapped = jnp.where(first_half,
                            pltpu.roll(p, LANES - HEAD_DIM // 2, 1),
                            pltpu.roll(p, HEAD_DIM // 2, 1))
        return rinv * (p * rope_a[kind] + swapped * rope_b[kind])

    n_a = QKV_A // LANES
    kinds = [0] * 4 + [1] * 4 + [None] * 4 + [2] * 4 + [3, None]
    b_outs = [(qb_ref, i) for i in range(4)] + [(kb_ref, 0), (vb_ref, 0)]

    def epilogue(blk, val):
        if blk >= n_a:
            out_ref, i = b_outs[blk - n_a]
            out_ref[:, i * LANES:(i + 1) * LANES] = val.astype(BF16)
            return
        a1_ref[:, blk * LANES:(blk + 1) * LANES] = val.astype(BF16)
        stage1_ref[blk] = val
        for r4 in range(4):
            rows4 = stage1_ref[blk, pl.ds(r4, tm // 4, stride=4), :]
            a4_ref[:, r4 * QKV_A + blk * LANES:r4 * QKV_A + (blk + 1) * LANES] = rows4.astype(BF16)
            stage4_ref[blk, r4] = rows4
        for r4 in range(4):
            for j in range(4):
                rows16 = stage4_ref[blk, r4, pl.ds(j, tm // 16, stride=4), :]
                c0 = (r4 + 4 * j) * QKV_A + blk * LANES
                a16_ref[:, c0:c0 + LANES] = rows16.astype(BF16)

    chunk = MXU_COLS
    bounds = [(c0, min(c0 + chunk, w_ref.shape[1])) for c0 in range(0, w_ref.shape[1], chunk)]

    def project(i, _):
        c0, c1 = bounds[i]
        return jnp.dot(h, w_ref[:, c0:c1], preferred_element_type=F32)

    def finish(i, proj):
        c0, c1 = bounds[i]
        blks = list(range(c0 // LANES, c1 // LANES))
        ps = {blk: proj[:, (blk - blks[0]) * LANES:(blk - blks[0] + 1) * LANES] for blk in blks}
        normed = [blk for blk in blks if kinds[blk] is not None]
        per_pass = MXU_COLS // LANES
        for g in range(0, len(normed), per_pass):
            grp = normed[g:g + per_pass]
            for blk, ss in zip(grp, sum_squares([ps[blk] for blk in grp])):
                ps[blk] = norm_rope(ps[blk], ss, kinds[blk])
        for blk in blks:
            epilogue(blk, ps[blk])

    _software_pipeline(len(bounds), [project, finish])


def _in_proj(x, attn_norm, w_in_bf16, cos_t, sin_t, gains):
    b, s, d = x.shape
    tm = _pick_tile(s, TOKEN_TILE)
    tok = lambda rows, cols: pl.BlockSpec((None, rows, cols), lambda i, n: (i, n, 0))
    tab = pl.BlockSpec((tm, LANES), lambda i, n: (n, 0))
    outs = [(tm // dl, s // dl, dl * QKV_A) for dl in (1, 4, 16)] + [(tm, s, QB), (tm, s, KB), (tm, s, KB)]
    return pl.pallas_call(
        _in_proj_kernel,
        grid=(b, s // tm),
        in_specs=[tok(tm, d), _const_spec((1, d)), _const_spec(w_in_bf16.shape), tab, tab,
                  _const_spec(gains.shape)],
        out_specs=[tok(rows, cols) for rows, _, cols in outs],
        out_shape=[jax.ShapeDtypeStruct((b, ln, cols), BF16) for _, ln, cols in outs],
        scratch_shapes=[pltpu.VMEM((QKV_A // LANES, tm, LANES), F32),
                        pltpu.VMEM((QKV_A // LANES, 4, tm // 4, LANES), F32)],
        compiler_params=pltpu.CompilerParams(
            dimension_semantics=("parallel", "parallel"), vmem_limit_bytes=VMEM_LIMIT_BYTES),
        name="in_proj",
    )(x, attn_norm.reshape(1, d), w_in_bf16, cos_t, sin_t, gains)


def _band_bias_t(w):
    nk = Q_BLOCK + 2 * w
    cc = jnp.arange(nk)[:, None]
    aa = jnp.arange(Q_BLOCK)[None, :]
    band = (cc >= aa) & (cc <= aa + 2 * w)
    variants = [band & ((cc >= w) | (e & 1 == 0)) & ((cc < nk - w) | (e & 2 == 0)) for e in range(4)]
    return jnp.where(jnp.stack(variants), 0.0, NEG_BIG).astype(BF16)


def _window_rows(left_ref, mid_ref, right_ref, j, w, cols):
    tq = mid_ref.shape[0]
    lo, hi = j * Q_BLOCK - w, (j + 1) * Q_BLOCK + w
    parts = []
    if lo < 0:
        parts.append(left_ref[w + lo:w, cols])
    parts.append(mid_ref[max(lo, 0):min(hi, tq), cols])
    if hi > tq:
        parts.append(right_ref[0:hi - tq, cols])
    return parts[0] if len(parts) == 1 else jnp.concatenate(parts, axis=0)


def _mask_variant(n, n_tiles, j, nblk):
    first = (n == 0) & (j == 0)
    last = (n == n_tiles - 1) & (j == nblk - 1)
    return jnp.where(first, 1, 0) + jnp.where(last, 2, 0)


def _attention_items(items, sinks=None):
    bq = Q_BLOCK
    lo_q = lax.broadcasted_iota(jnp.int32, (bq, LANES), 1) < HEAD_DIM
    eye = (lax.broadcasted_iota(jnp.int32, (bq, bq), 0)
           == lax.broadcasted_iota(jnp.int32, (bq, bq), 1)).astype(BF16)
    onehot2 = jnp.concatenate([eye, eye], axis=0)

    def scores(i, _):
        bias_t = items[i]["bias"]()
        out = []
        for q, k in zip(items[i]["q"](), items[i]["k"]()):
            zero = jnp.zeros_like(q)
            q2 = jnp.concatenate([jnp.where(lo_q, q, zero), jnp.where(lo_q, zero, q)], axis=0)
            lhs = jnp.concatenate([q2, onehot2], axis=1)
            k_aug = jnp.concatenate([k, bias_t], axis=1)
            out.append(lax.dot_general(lhs, k_aug, (((1,), (1,)), ((), ())), preferred_element_type=F32))
        return out

    def softmax(i, sc):
        maxes = [jnp.max(s, axis=-1, keepdims=True) for s in sc]
        if sinks is not None:
            upper = lax.broadcasted_iota(jnp.int32, (2 * bq, 1), 0) >= bq
            maxes = [jnp.maximum(m, jnp.where(upper, so, se)) for m, (se, so) in zip(maxes, sinks)]
        return maxes, [jnp.exp2(s - m).astype(BF16) for s, m in zip(sc, maxes)]

    def weighted_values(i, st):
        maxes, probs = st
        res, augmented = [], {}
        for p, v in zip(probs, items[i]["v"]()):
            if id(v) not in augmented:
                lo_v = lax.broadcasted_iota(jnp.int32, v.shape, 1) < HEAD_DIM
                one = jnp.ones_like(v)
                augmented[id(v)] = (jnp.where(lo_v, v, one), jnp.where(lo_v, one, v))
            v_lo, v_hi = augmented[id(v)]
            res.append((jnp.dot(p[:bq], v_lo, preferred_element_type=F32),
                        jnp.dot(p[bq:], v_hi, preferred_element_type=F32)))
        return maxes, res

    def finish(i, st):
        maxes, res = st
        outs = []
        for k, ((re, ro), m) in enumerate(zip(res, maxes)):
            acc = jnp.where(lo_q, re, ro)
            l = pltpu.roll(jnp.where(lo_q, ro, re), HEAD_DIM, 1)
            m_rep = jnp.where(lo_q, m[:bq], m[bq:])
            if sinks is not None:
                l = l + jnp.exp2(jnp.where(lo_q, sinks[k][0], sinks[k][1]) - m_rep)
            outs.append((acc / l, m_rep + jnp.log2(l)))
        items[i]["store"](outs)
        return None

    _software_pipeline(len(items), [scores, softmax, weighted_values, finish])


def _dilated_attn_kernel(mid_ref, left_ref, right_ref, bias_ref, o_ref, lse_ref, *, w, n_tiles, n_res):
    tq = mid_ref.shape[0]
    nblk = tq // Q_BLOCK
    n = pl.program_id(2)
    n_pairs = N_HEADS_A // 2

    def item(res, j):
        rows = slice(j * Q_BLOCK, (j + 1) * Q_BLOCK)
        base = res * QKV_A

        def store(outs):
            for cs, (o, lse) in zip(_lane_blocks(n_pairs, res * QA), outs):
                o_ref[rows, cs] = o.astype(BF16)
                lse_ref[rows, cs] = lse

        window = lambda c0: [_window_rows(left_ref, mid_ref, right_ref, j, w, cs)
                             for cs in _lane_blocks(n_pairs, base + c0)]
        return dict(q=lambda: [mid_ref[rows, cs] for cs in _lane_blocks(n_pairs, base)],
                    k=lambda: window(QA), v=lambda: window(2 * QA),
                    bias=lambda: bias_ref[_mask_variant(n, n_tiles, j, nblk)], store=store)

    _attention_items([item(res, j) for res in range(n_res) for j in range(nblk)])


def _dilated_attn(qkv, dilation, half_window, bias_t):
    b, ln, _ = qkv.shape
    w = half_window
    tq = _pick_tile(ln, ATTN_TILE)
    n_res = min(dilation, ATTN_TILE // tq)
    n_tiles = ln // tq
    per = tq // w
    last_halo = ln // w - 1
    in_cols = n_res * QKV_A
    return pl.pallas_call(
        functools.partial(_dilated_attn_kernel, w=w, n_tiles=n_tiles, n_res=n_res),
        grid=(b, dilation // n_res, n_tiles),
        in_specs=[pl.BlockSpec((None, tq, in_cols), lambda i, r, n: (i, n, r)),
                  pl.BlockSpec((None, w, in_cols), lambda i, r, n: (i, jnp.maximum(n * per - 1, 0), r)),
                  pl.BlockSpec((None, w, in_cols), lambda i, r, n: (i, jnp.minimum((n + 1) * per, last_halo), r)),
                  _const_spec(bias_t.shape)],
        out_specs=[pl.BlockSpec((None, tq, n_res * QA), lambda i, r, n: (i, n, r))] * 2,
        out_shape=[jax.ShapeDtypeStruct((b, ln, dilation * QA), BF16),
                   jax.ShapeDtypeStruct((b, ln, dilation * QA), F32)],
        compiler_params=pltpu.CompilerParams(
            dimension_semantics=("parallel", "parallel", "parallel"),
            vmem_limit_bytes=VMEM_LIMIT_BYTES),
        name=f"dilated_attn_d{dilation}",
    )(qkv, qkv, qkv, bias_t)


def _swa_kernel(sink_ref, q_ref, km_ref, kl_ref, kr_ref, vm_ref, vl_ref, vr_ref, bias_ref, o_ref,
                *, w, n_tiles):
    tq = q_ref.shape[0]
    nblk = tq // Q_BLOCK
    n = pl.program_id(1)
    group = N_HEADS_B // N_KV_B
    all_lanes = slice(0, LANES)
    sinks = [(sink_ref[p] * LOG2_E, sink_ref[group + p] * LOG2_E) for p in range(group)]

    def item(j):
        rows = slice(j * Q_BLOCK, (j + 1) * Q_BLOCK)

        def store(outs):
            for cs, (o, _) in zip(_lane_blocks(group), outs):
                o_ref[rows, cs] = o.astype(BF16)

        return dict(q=lambda: [q_ref[rows, cs] for cs in _lane_blocks(group)],
                    k=lambda: [_window_rows(kl_ref, km_ref, kr_ref, j, w, all_lanes)] * group,
                    v=lambda: [_window_rows(vl_ref, vm_ref, vr_ref, j, w, all_lanes)] * group,
                    bias=lambda: bias_ref[_mask_variant(n, n_tiles, j, nblk)], store=store)

    _attention_items([item(j) for j in range(nblk)], sinks)


def _swa_attn(q, k, v, sink, half_window, bias_t):
    b, s, cols = q.shape
    w = half_window
    tq = _pick_tile(s, ATTN_TILE)
    n_tiles = s // tq
    per = tq // w
    last_halo = s // w - 1
    kv_specs = [pl.BlockSpec((None, tq, KB), lambda i, n: (i, n, 0)),
                pl.BlockSpec((None, w, KB), lambda i, n: (i, jnp.maximum(n * per - 1, 0), 0)),
                pl.BlockSpec((None, w, KB), lambda i, n: (i, jnp.minimum((n + 1) * per, last_halo), 0))]
    return pl.pallas_call(
        functools.partial(_swa_kernel, w=w, n_tiles=n_tiles),
        grid=(b, n_tiles),
        in_specs=[pl.BlockSpec(memory_space=pltpu.SMEM),
                  pl.BlockSpec((None, tq, cols), lambda i, n: (i, n, 0))] + kv_specs + kv_specs
                 + [_const_spec(bias_t.shape)],
        out_specs=pl.BlockSpec((None, tq, cols), lambda i, n: (i, n, 0)),
        out_shape=jax.ShapeDtypeStruct((b, s, cols), BF16),
        compiler_params=pltpu.CompilerParams(
            dimension_semantics=("parallel", "parallel"), vmem_limit_bytes=VMEM_LIMIT_BYTES),
        name="swa_attn",
    )(sink, q, k, k, k, v, v, v, bias_t)


def _out_ffn_kernel(x_ref, o1_ref, o4_ref, o16_ref, l1_ref, l4_ref, l16_ref, ob_ref,
                    wo_ref, g_ref, wg_ref, wu_ref, wd_ref, y_ref,
                    x1_ref, act_ref, lnat_ref, onat_ref, *, ff_chunk):
    tm = x_ref.shape[0]
    for slot, (d, l_ref, o_ref) in enumerate(((4, l4_ref, o4_ref), (16, l16_ref, o16_ref))):
        for res in range(d):
            rows = pl.ds(res, tm // d, stride=d)
            for blk, cs in enumerate(_lane_blocks(QA // LANES, res * QA)):
                lnat_ref[slot, blk, rows, :] = l_ref[:, cs]
                onat_ref[slot, blk, rows, :] = o_ref[:, cs].astype(F32)

    gather = lambda ref, slot: jnp.concatenate([ref[slot, blk] for blk in range(QA // LANES)], axis=1)
    l1, l2, l3 = l1_ref[...], gather(lnat_ref, 0), gather(lnat_ref, 1)
    mx = jnp.maximum(jnp.maximum(l1, l2), l3)
    e1, e2, e3 = jnp.exp2(l1 - mx), jnp.exp2(l2 - mx), jnp.exp2(l3 - mx)
    mixed = e1 * o1_ref[...].astype(F32) + e2 * gather(onat_ref, 0) + e3 * gather(onat_ref, 1)
    out_a = (mixed / (e1 + e2 + e3)).astype(BF16)
    x1 = (x_ref[...]
          + jnp.dot(out_a, wo_ref[0:QA, :], preferred_element_type=F32)
          + jnp.dot(ob_ref[...], wo_ref[QA:QA + QB, :], preferred_element_type=F32))
    ms = jnp.mean(x1 * x1, axis=-1, keepdims=True)
    h2 = (x1 * lax.rsqrt(ms + EPS) * g_ref[...]).astype(BF16)
    x1_ref[...] = x1

    def gate_up(i, _):
        cs = slice(i * ff_chunk, (i + 1) * ff_chunk)
        return (jnp.dot(h2, wg_ref[:, cs], preferred_element_type=F32),
                jnp.dot(h2, wu_ref[:, cs], preferred_element_type=F32))

    def swiglu(i, gu):
        gate, up = gu
        act_ref[:, i * ff_chunk:(i + 1) * ff_chunk] = (gate / (1.0 + jnp.exp(-gate)) * up).astype(BF16)

    _software_pipeline(wg_ref.shape[1] // ff_chunk, [gate_up, swiglu])
    y_ref[...] = x1_ref[...] + jnp.dot(act_ref[...], wd_ref[...], preferred_element_type=F32)


def _out_ffn(x, o_branches, lse_branches, ob, wo, ffn_norm, wg, wu, wd):
    b, s, d = x.shape
    tm = _pick_tile(s, TOKEN_TILE)
    tok = lambda rows, cols: pl.BlockSpec((None, rows, cols), lambda i, n: (i, n, 0))
    dils = [dl for _, dl in DILATED_PATTERNS]
    consts = (wo, ffn_norm.reshape(1, d), wg, wu, wd)
    return pl.pallas_call(
        functools.partial(_out_ffn_kernel, ff_chunk=MXU_COLS),
        grid=(b, s // tm),
        in_specs=([tok(tm, d)] + [tok(tm // dl, dl * QA) for dl in dils] * 2 + [tok(tm, QB)]
                  + [_const_spec(t.shape) for t in consts]),
        out_specs=tok(tm, d),
        out_shape=jax.ShapeDtypeStruct((b, s, d), F32),
        scratch_shapes=[pltpu.VMEM((tm, d), F32), pltpu.VMEM((tm, wg.shape[1]), BF16),
                        pltpu.VMEM((2, QA // LANES, tm, LANES), F32),
                        pltpu.VMEM((2, QA // LANES, tm, LANES), F32)],
        compiler_params=pltpu.CompilerParams(
            dimension_semantics=("parallel", "parallel"), vmem_limit_bytes=VMEM_LIMIT_BYTES),
        name="out_ffn",
    )(x, *o_branches, *lse_branches, ob, *consts)


def _swa_head_order():
    group = N_HEADS_B // N_KV_B
    return [h for p in range(group) for h in (p, group + p)]


def _head_cols(heads):
    return jnp.concatenate([jnp.arange(h * HEAD_DIM, (h + 1) * HEAD_DIM) for h in heads])


def _rope_tables(s):
    half = HEAD_DIM // 2
    inv = ROPE_THETA ** (-2.0 * jnp.arange(half, dtype=F32) / HEAD_DIM)
    ang = jnp.arange(s).astype(F32)[:, None] * inv[None, :]
    cos, sin = jnp.cos(ang), jnp.sin(ang)
    reps = LANES // HEAD_DIM
    return (jnp.tile(jnp.concatenate([cos, cos], axis=-1), (1, reps)),
            jnp.tile(jnp.concatenate([-sin, sin], axis=-1), (1, reps)))


def _gain_rows(qnorm_a, knorm_a, qnorm_b, knorm_b):
    half = HEAD_DIM // 2
    scale = HEAD_DIM ** -0.5 * LOG2_E
    rows = [qnorm_a * scale, knorm_a, qnorm_b * scale, knorm_b]
    swapped = [jnp.concatenate([g[half:], g[:half]]) for g in rows]
    return jnp.stack([jnp.tile(g.astype(F32), LANES // HEAD_DIM) for g in rows + swapped])


def _encoder_layer(x, tables, prm):
    cos_t, sin_t = tables
    a1, a4, a16, qb, kb, vb = _in_proj(x, prm["attn_norm"], prm["w_in"], cos_t, sin_t, prm["gains"])
    branches = [_dilated_attn(qkv, dilation, (window // 2) // dilation, prm["bias_a"])
                for qkv, (window, dilation) in zip((a1, a4, a16), DILATED_PATTERNS)]
    ob = _swa_attn(qb, kb, vb, prm["sink"], SWA_HALF_WINDOW, prm["bias_b"])
    return _out_ffn(x, [o for o, _ in branches], [l for _, l in branches], ob,
                    prm["w_out"], prm["ffn_norm"], prm["w_gate"], prm["w_up"], prm["w_down"])


def kernel(x_prompt, x_sample, attn_norm, w_in, qnorm_a, knorm_a, qnorm_b, knorm_b, sink_b, w_out,
           ffn_norm, w_gate, w_up, w_down):
    depth = w_in.shape[0]
    order = _swa_head_order()
    in_cols = jnp.concatenate([jnp.arange(QKV_A), QKV_A + _head_cols(order),
                               jnp.arange(QKV_A + QB, w_in.shape[-1])])
    out_rows = jnp.concatenate([jnp.arange(QA), QA + _head_cols(order)])
    half_windows = {(window // 2) // dilation for window, dilation in DILATED_PATTERNS}
    assert len(half_windows) == 1
    bias_a = _band_bias_t(half_windows.pop())
    bias_b = _band_bias_t(SWA_HALF_WINDOW)
    tables = {x.shape[1]: _rope_tables(x.shape[1]) for x in (x_prompt, x_sample)}
    ys = [x_prompt, x_sample]
    for i in range(depth):
        prm = dict(
            attn_norm=attn_norm[i], ffn_norm=ffn_norm[i],
            w_in=w_in[i][:, in_cols].astype(BF16),
            w_out=w_out[i][out_rows, :].astype(BF16),
            w_gate=w_gate[i].astype(BF16), w_up=w_up[i].astype(BF16), w_down=w_down[i].astype(BF16),
            gains=_gain_rows(qnorm_a[i], knorm_a[i], qnorm_b[i], knorm_b[i]),
            sink=sink_b[i].astype(F32), bias_a=bias_a, bias_b=bias_b)
        ys = [_encoder_layer(y, tables[y.shape[1]], prm) for y in ys]
    return tuple(ys)
```

```python
import functools
import math

import jax
import jax.numpy as jnp
from jax import lax
from jax.experimental import pallas as pl
from jax.experimental.pallas import tpu as pltpu

HEAD_DIM = 64
N_HEADS_A = 8
N_HEADS_B = 8
N_KV_B = 2
DILATED_PATTERNS = ((128, 1), (512, 4), (2048, 16))
SWA_HALF_WINDOW = 128
ROPE_THETA = 10000.0
EPS = 1e-6

QA = N_HEADS_A * HEAD_DIM
QB = N_HEADS_B * HEAD_DIM
KB = N_KV_B * HEAD_DIM
QKV_A = 3 * QA
LANES = 128
MXU_COLS = 256
N_MXU = 2
Q_BLOCK = 128
TOKEN_TILE = 512
ATTN_TILE = 1024
NEG_BIG = -1e30
VMEM_LIMIT_BYTES = 56 * 1024 * 1024
LOG2_E = math.log2(math.e)

BF16 = jnp.bfloat16
F32 = jnp.float32


def _pick_tile(n, cap):
    t = cap
    while n % t:
        t //= 2
    return t


def _const_spec(shape):
    return pl.BlockSpec(shape, lambda *_: (0,) * len(shape), pipeline_mode=pl.Buffered(1))


def _lane_blocks(n, start=0):
    return [slice(start + i * LANES, start + (i + 1) * LANES) for i in range(n)]


def _software_pipeline(n_items, stages):
    states = [None] * n_items
    for t in range(n_items + len(stages) - 1):
        for s, stage in enumerate(stages):
            if 0 <= t - s < n_items:
                states[t - s] = stage(t - s, states[t - s])


def _in_proj_kernel(x_ref, g_ref, w_ref, cos_ref, sin_ref, gains_ref,
                    a1_ref, a4_ref, a16_ref, qb_ref, kb_ref, vb_ref, stage1_ref, stage4_ref):
    tm = x_ref.shape[0]
    x = x_ref[...]
    ms = jnp.mean(x * x, axis=-1, keepdims=True)
    h = (x * lax.rsqrt(ms + EPS) * g_ref[...]).astype(BF16)

    r = lax.broadcasted_iota(jnp.int32, (MXU_COLS, MXU_COLS), 0)
    c = lax.broadcasted_iota(jnp.int32, (MXU_COLS, MXU_COLS), 1)
    ones_blk = ((r // HEAD_DIM) == (c // HEAD_DIM)).astype(BF16)
    lane = lax.broadcasted_iota(jnp.int32, (tm, LANES), 1)
    first_half = (lane % HEAD_DIM) < (HEAD_DIM // 2)

    cos = cos_ref[...]
    sin = sin_ref[...]
    rope_a = [gains_ref[k:k + 1, :] * cos for k in range(4)]
    rope_b = [gains_ref[4 + k:5 + k, :] * sin for k in range(4)]

    def sum_squares(ps):
        sq = jnp.concatenate([(p * p).astype(BF16) for p in ps], axis=1)
        width = sq.shape[1]
        ss = jnp.dot(sq, ones_blk[:width, :width], preferred_element_type=F32)
        return [ss[:, k * LANES:(k + 1) * LANES] for k in range(len(ps))]

    def norm_rope(p, ss, kind):
        rinv = lax.rsqrt(ss * (1.0 / HEAD_DIM) + EPS)
        swapped = jnp.where(first_half,
                            pltpu.roll(p, LANES - HEAD_DIM // 2, 1),
                            pltpu.roll(p, HEAD_DIM // 2, 1))
        return rinv * (p * rope_a[kind] + swapped * rope_b[kind])

    n_a = QKV_A // LANES
    kinds = [0] * 4 + [1] * 4 + [None] * 4 + [2] * 4 + [3, None]
    b_outs = [(qb_ref, i) for i in range(4)] + [(kb_ref, 0), (vb_ref, 0)]

    def epilogue(blk, val):
        if blk >= n_a:
            out_ref, i = b_outs[blk - n_a]
            out_ref[:, i * LANES:(i + 1) * LANES] = val.astype(BF16)
            return
        a1_ref[:, blk * LANES:(blk + 1) * LANES] = val.astype(BF16)
        stage1_ref[blk] = val
        for r4 in range(4):
            rows4 = stage1_ref[blk, pl.ds(r4, tm // 4, stride=4), :]
            a4_ref[:, r4 * QKV_A + blk * LANES:r4 * QKV_A + (blk + 1) * LANES] = rows4.astype(BF16)
            stage4_ref[blk, r4] = rows4
        for r4 in range(4):
            for j in range(4):
                rows16 = stage4_ref[blk, r4, pl.ds(j, tm // 16, stride=4), :]
                c0 = (r4 + 4 * j) * QKV_A + blk * LANES
                a16_ref[:, c0:c0 + LANES] = rows16.astype(BF16)

    chunk = MXU_COLS
    bounds = [(c0, min(c0 + chunk, w_ref.shape[1])) for c0 in range(0, w_ref.shape[1], chunk)]

    def project(i, _):
        c0, c1 = bounds[i]
        return jnp.dot(h, w_ref[:, c0:c1], preferred_element_type=F32)

    def finish(i, proj):
        c0, c1 = bounds[i]
        blks = list(range(c0 // LANES, c1 // LANES))
        ps = {blk: proj[:, (blk - blks[0]) * LANES:(blk - blks[0] + 1) * LANES] for blk in blks}
        normed = [blk for blk in blks if kinds[blk] is not None]
        per_pass = MXU_COLS // LANES
        for g in range(0, len(normed), per_pass):
            grp = normed[g:g + per_pass]
            for blk, ss in zip(grp, sum_squares([ps[blk] for blk in grp])):
                ps[blk] = norm_rope(ps[blk], ss, kinds[blk])
        for blk in blks:
            epilogue(blk, ps[blk])

    _software_pipeline(len(bounds), [project, finish])


def _in_proj(x, attn_norm, w_in_bf16, cos_t, sin_t, gains):
    b, s, d = x.shape
    tm = _pick_tile(s, TOKEN_TILE)
    tok = lambda rows, cols: pl.BlockSpec((None, rows, cols), lambda i, n: (i, n, 0))
    tab = pl.BlockSpec((tm, LANES), lambda i, n: (n, 0))
    outs = [(tm // dl, s // dl, dl * QKV_A) for dl in (1, 4, 16)] + [(tm, s, QB), (tm, s, KB), (tm, s, KB)]
    return pl.pallas_call(
        _in_proj_kernel,
        grid=(b, s // tm),
        in_specs=[tok(tm, d), _const_spec((1, d)), _const_spec(w_in_bf16.shape), tab, tab,
                  _const_spec(gains.shape)],
        out_specs=[tok(rows, cols) for rows, _, cols in outs],
        out_shape=[jax.ShapeDtypeStruct((b, ln, cols), BF16) for _, ln, cols in outs],
        scratch_shapes=[pltpu.VMEM((QKV_A // LANES, tm, LANES), F32),
                        pltpu.VMEM((QKV_A // LANES, 4, tm // 4, LANES), F32)],
        compiler_params=pltpu.CompilerParams(
            dimension_semantics=("parallel", "parallel"), vmem_limit_bytes=VMEM_LIMIT_BYTES),
        name="in_proj",
    )(x, attn_norm.reshape(1, d), w_in_bf16, cos_t, sin_t, gains)


def _band_bias_t(w):
    nk = Q_BLOCK + 2 * w
    cc = jnp.arange(nk)[:, None]
    aa = jnp.arange(Q_BLOCK)[None, :]
    band = (cc >= aa) & (cc <= aa + 2 * w)
    variants = [band & ((cc >= w) | (e & 1 == 0)) & ((cc < nk - w) | (e & 2 == 0)) for e in range(4)]
    return jnp.where(jnp.stack(variants), 0.0, NEG_BIG).astype(BF16)


def _window_rows(left_ref, mid_ref, right_ref, j, w, cols):
    tq = mid_ref.shape[0]
    lo, hi = j * Q_BLOCK - w, (j + 1) * Q_BLOCK + w
    parts = []
    if lo < 0:
        parts.append(mid_ref[0:-lo, cols] if left_ref is None else left_ref[w + lo:w, cols])
    parts.append(mid_ref[max(lo, 0):min(hi, tq), cols])
    if hi > tq:
        parts.append(mid_ref[0:hi - tq, cols] if right_ref is None else right_ref[0:hi - tq, cols])
    return parts[0] if len(parts) == 1 else jnp.concatenate(parts, axis=0)


def _mask_variant(n, n_tiles, j, nblk):
    first = (n == 0) & (j == 0)
    last = (n == n_tiles - 1) & (j == nblk - 1)
    return jnp.where(first, 1, 0) + jnp.where(last, 2, 0)


def _attention_items(items, sinks=None):
    bq = Q_BLOCK
    lo_q = lax.broadcasted_iota(jnp.int32, (bq, LANES), 1) < HEAD_DIM
    eye = (lax.broadcasted_iota(jnp.int32, (bq, bq), 0)
           == lax.broadcasted_iota(jnp.int32, (bq, bq), 1)).astype(BF16)
    onehot2 = jnp.concatenate([eye, eye], axis=0)

    def scores(i, _):
        bias_t = items[i]["bias"]()
        out = []
        for q, k in zip(items[i]["q"](), items[i]["k"]()):
            zero = jnp.zeros_like(q)
            q2 = jnp.concatenate([jnp.where(lo_q, q, zero), jnp.where(lo_q, zero, q)], axis=0)
            lhs = jnp.concatenate([q2, onehot2], axis=1)
            k_aug = jnp.concatenate([k, bias_t], axis=1)
            out.append(lax.dot_general(lhs, k_aug, (((1,), (1,)), ((), ())), preferred_element_type=F32))
        return out

    def softmax(i, sc):
        maxes = [jnp.max(s, axis=-1, keepdims=True) for s in sc]
        if sinks is not None:
            upper = lax.broadcasted_iota(jnp.int32, (2 * bq, 1), 0) >= bq
            maxes = [jnp.maximum(m, jnp.where(upper, so, se)) for m, (se, so) in zip(maxes, sinks)]
        return maxes, [jnp.exp2(s - m).astype(BF16) for s, m in zip(sc, maxes)]

    def weighted_values(i, st):
        maxes, probs = st
        res, augmented = [], {}
        for p, v in zip(probs, items[i]["v"]()):
            if id(v) not in augmented:
                lo_v = lax.broadcasted_iota(jnp.int32, v.shape, 1) < HEAD_DIM
                one = jnp.ones_like(v)
                augmented[id(v)] = (jnp.where(lo_v, v, one), jnp.where(lo_v, one, v))
            v_lo, v_hi = augmented[id(v)]
            res.append((jnp.dot(p[:bq], v_lo, preferred_element_type=F32),
                        jnp.dot(p[bq:], v_hi, preferred_element_type=F32)))
        return maxes, res

    def finish(i, st):
        maxes, res = st
        outs = []
        for k, ((re, ro), m) in enumerate(zip(res, maxes)):
            acc = jnp.where(lo_q, re, ro)
            l = pltpu.roll(jnp.where(lo_q, ro, re), HEAD_DIM, 1)
            m_rep = jnp.where(lo_q, m[:bq], m[bq:])
            if sinks is not None:
                l = l + jnp.exp2(jnp.where(lo_q, sinks[k][0], sinks[k][1]) - m_rep)
            outs.append((acc / l, m_rep + jnp.log2(l)))
        items[i]["store"](outs)
        return None

    _software_pipeline(len(items), [scores, softmax, weighted_values, finish])


def _dilated_attn_kernel(mid_ref, *refs, w, n_tiles, n_res):
    left_ref, right_ref = refs[:-3] if n_tiles > 1 else (None, None)
    bias_ref, o_ref, lse_ref = refs[-3:]
    tq = mid_ref.shape[0]
    nblk = tq // Q_BLOCK
    n = pl.program_id(2)
    n_pairs = N_HEADS_A // 2

    def item(res, j):
        rows = slice(j * Q_BLOCK, (j + 1) * Q_BLOCK)
        base = res * QKV_A

        def store(outs):
            for cs, (o, lse) in zip(_lane_blocks(n_pairs, res * QA), outs):
                o_ref[rows, cs] = o.astype(BF16)
                lse_ref[rows, cs] = lse

        window = lambda c0: [_window_rows(left_ref, mid_ref, right_ref, j, w, cs)
                             for cs in _lane_blocks(n_pairs, base + c0)]
        return dict(q=lambda: [mid_ref[rows, cs] for cs in _lane_blocks(n_pairs, base)],
                    k=lambda: window(QA), v=lambda: window(2 * QA),
                    bias=lambda: bias_ref[_mask_variant(n, n_tiles, j, nblk)], store=store)

    _attention_items([item(res, j) for res in range(n_res) for j in range(nblk)])


def _dilated_attn(qkv, dilation, half_window, bias_t):
    b, ln, _ = qkv.shape
    w = half_window
    tq = _pick_tile(ln, ATTN_TILE)
    n_res = min(dilation, ATTN_TILE // tq)
    n_tiles = ln // tq
    per = tq // w
    last_halo = ln // w - 1
    in_cols = n_res * QKV_A
    halo_specs = [
        pl.BlockSpec((None, w, in_cols), lambda i, r, n: (i, jnp.maximum(n * per - 1, 0), r)),
        pl.BlockSpec((None, w, in_cols), lambda i, r, n: (i, jnp.minimum((n + 1) * per, last_halo), r)),
    ] if n_tiles > 1 else []
    return pl.pallas_call(
        functools.partial(_dilated_attn_kernel, w=w, n_tiles=n_tiles, n_res=n_res),
        grid=(b, dilation // n_res, n_tiles),
        in_specs=([pl.BlockSpec((None, tq, in_cols), lambda i, r, n: (i, n, r))] + halo_specs
                  + [_const_spec(bias_t.shape)]),
        out_specs=[pl.BlockSpec((None, tq, n_res * QA), lambda i, r, n: (i, n, r))] * 2,
        out_shape=[jax.ShapeDtypeStruct((b, ln, dilation * QA), BF16),
                   jax.ShapeDtypeStruct((b, ln, dilation * QA), F32)],
        compiler_params=pltpu.CompilerParams(
            dimension_semantics=("parallel", "parallel", "parallel"),
            vmem_limit_bytes=VMEM_LIMIT_BYTES),
        name=f"dilated_attn_d{dilation}",
    )(*([qkv] * (1 + len(halo_specs))), bias_t)


def _swa_kernel(sink_ref, q_ref, km_ref, kl_ref, kr_ref, vm_ref, vl_ref, vr_ref, bias_ref, o_ref,
                *, w, n_tiles):
    tq = q_ref.shape[0]
    nblk = tq // Q_BLOCK
    n = pl.program_id(1)
    group = N_HEADS_B // N_KV_B
    all_lanes = slice(0, LANES)
    sinks = [(sink_ref[p] * LOG2_E, sink_ref[group + p] * LOG2_E) for p in range(group)]

    def item(j):
        rows = slice(j * Q_BLOCK, (j + 1) * Q_BLOCK)

        def store(outs):
            for cs, (o, _) in zip(_lane_blocks(group), outs):
                o_ref[rows, cs] = o.astype(BF16)

        return dict(q=lambda: [q_ref[rows, cs] for cs in _lane_blocks(group)],
                    k=lambda: [_window_rows(kl_ref, km_ref, kr_ref, j, w, all_lanes)] * group,
                    v=lambda: [_window_rows(vl_ref, vm_ref, vr_ref, j, w, all_lanes)] * group,
                    bias=lambda: bias_ref[_mask_variant(n, n_tiles, j, nblk)], store=store)

    _attention_items([item(j) for j in range(nblk)], sinks)


def _swa_attn(q, k, v, sink, half_window, bias_t):
    b, s, cols = q.shape
    w = half_window
    tq = _pick_tile(s, ATTN_TILE)
    n_tiles = s // tq
    per = tq // w
    last_halo = s // w - 1
    kv_specs = [pl.BlockSpec((None, tq, KB), lambda i, n: (i, n, 0)),
                pl.BlockSpec((None, w, KB), lambda i, n: (i, jnp.maximum(n * per - 1, 0), 0)),
                pl.BlockSpec((None, w, KB), lambda i, n: (i, jnp.minimum((n + 1) * per, last_halo), 0))]
    return pl.pallas_call(
        functools.partial(_swa_kernel, w=w, n_tiles=n_tiles),
        grid=(b, n_tiles),
        in_specs=[pl.BlockSpec(memory_space=pltpu.SMEM),
                  pl.BlockSpec((None, tq, cols), lambda i, n: (i, n, 0))] + kv_specs + kv_specs
                 + [_const_spec(bias_t.shape)],
        out_specs=pl.BlockSpec((None, tq, cols), lambda i, n: (i, n, 0)),
        out_shape=jax.ShapeDtypeStruct((b, s, cols), BF16),
        compiler_params=pltpu.CompilerParams(
            dimension_semantics=("parallel", "parallel"), vmem_limit_bytes=VMEM_LIMIT_BYTES),
        name="swa_attn",
    )(sink, q, k, k, k, v, v, v, bias_t)


def _out_ffn_kernel(x_ref, o1_ref, o4_ref, o16_ref, l1_ref, l4_ref, l16_ref, ob_ref,
                    wo_ref, g_ref, wg_ref, wu_ref, wd_ref, y_ref,
                    x1_ref, act_ref, lnat_ref, onat_ref, *, ff_chunk):
    tm = x_ref.shape[0]
    for slot, (d, l_ref, o_ref) in enumerate(((4, l4_ref, o4_ref), (16, l16_ref, o16_ref))):
        for res in range(d):
            rows = pl.ds(res, tm // d, stride=d)
            for blk, cs in enumerate(_lane_blocks(QA // LANES, res * QA)):
                lnat_ref[slot, blk, rows, :] = l_ref[:, cs]
                onat_ref[slot, blk, rows, :] = o_ref[:, cs].astype(F32)

    gather = lambda ref, slot: jnp.concatenate([ref[slot, blk] for blk in range(QA // LANES)], axis=1)
    l1, l2, l3 = l1_ref[...], gather(lnat_ref, 0), gather(lnat_ref, 1)
    mx = jnp.maximum(jnp.maximum(l1, l2), l3)
    e1, e2, e3 = jnp.exp2(l1 - mx), jnp.exp2(l2 - mx), jnp.exp2(l3 - mx)
    mixed = e1 * o1_ref[...].astype(F32) + e2 * gather(onat_ref, 0) + e3 * gather(onat_ref, 1)
    out_a = (mixed / (e1 + e2 + e3)).astype(BF16)
    x1 = (x_ref[...]
          + jnp.dot(out_a, wo_ref[0:QA, :], preferred_element_type=F32)
          + jnp.dot(ob_ref[...], wo_ref[QA:QA + QB, :], preferred_element_type=F32))
    ms = jnp.mean(x1 * x1, axis=-1, keepdims=True)
    h2 = (x1 * lax.rsqrt(ms + EPS) * g_ref[...]).astype(BF16)
    x1_ref[...] = x1

    def gate_up(i, _):
        cs = slice(i * ff_chunk, (i + 1) * ff_chunk)
        return (jnp.dot(h2, wg_ref[:, cs], preferred_element_type=F32),
                jnp.dot(h2, wu_ref[:, cs], preferred_element_type=F32))

    def swiglu(i, gu):
        gate, up = gu
        act_ref[:, i * ff_chunk:(i + 1) * ff_chunk] = (gate / (1.0 + jnp.exp(-gate)) * up).astype(BF16)

    _software_pipeline(wg_ref.shape[1] // ff_chunk, [gate_up, swiglu])
    y_ref[...] = x1_ref[...] + jnp.dot(act_ref[...], wd_ref[...], preferred_element_type=F32)


def _out_ffn(x, o_branches, lse_branches, ob, wo, ffn_norm, wg, wu, wd):
    b, s, d = x.shape
    tm = _pick_tile(s, TOKEN_TILE)
    tok = lambda rows, cols: pl.BlockSpec((None, rows, cols), lambda i, n: (i, n, 0))
    dils = [dl for _, dl in DILATED_PATTERNS]
    consts = (wo, ffn_norm.reshape(1, d), wg, wu, wd)
    return pl.pallas_call(
        functools.partial(_out_ffn_kernel, ff_chunk=MXU_COLS),
        grid=(b, s // tm),
        in_specs=([tok(tm, d)] + [tok(tm // dl, dl * QA) for dl in dils] * 2 + [tok(tm, QB)]
                  + [_const_spec(t.shape) for t in consts]),
        out_specs=tok(tm, d),
        out_shape=jax.ShapeDtypeStruct((b, s, d), F32),
        scratch_shapes=[pltpu.VMEM((tm, d), F32), pltpu.VMEM((tm, wg.shape[1]), BF16),
                        pltpu.VMEM((2, QA // LANES, tm, LANES), F32),
                        pltpu.VMEM((2, QA // LANES, tm, LANES), F32)],
        compiler_params=pltpu.CompilerParams(
            dimension_semantics=("parallel", "parallel"), vmem_limit_bytes=VMEM_LIMIT_BYTES),
        name="out_ffn",
    )(x, *o_branches, *lse_branches, ob, *consts)


def _swa_head_order():
    group = N_HEADS_B // N_KV_B
    return [h for p in range(group) for h in (p, group + p)]


def _permute_heads(w, axis, start, heads):
    take = lambda lo, hi: lax.slice_in_dim(w, lo, hi, axis=axis)
    parts = ([take(0, start)]
             + [take(start + h * HEAD_DIM, start + (h + 1) * HEAD_DIM) for h in heads]
             + [take(start + len(heads) * HEAD_DIM, w.shape[axis])])
    return jnp.concatenate(parts, axis=axis)


def _rope_tables(s):
    half = HEAD_DIM // 2
    inv = ROPE_THETA ** (-2.0 * jnp.arange(half, dtype=F32) / HEAD_DIM)
    inv_lanes = jnp.tile(inv, LANES // half)
    sign = jnp.tile(jnp.concatenate([-jnp.ones(half, F32), jnp.ones(half, F32)]), LANES // HEAD_DIM)
    ang = jnp.arange(s).astype(F32)[:, None] * inv_lanes[None, :]
    return jnp.cos(ang), jnp.sin(ang) * sign[None, :]


def _gain_rows(qnorm_a, knorm_a, qnorm_b, knorm_b):
    half = HEAD_DIM // 2
    scale = HEAD_DIM ** -0.5 * LOG2_E
    rows = [qnorm_a * scale, knorm_a, qnorm_b * scale, knorm_b]
    swapped = [jnp.concatenate([g[half:], g[:half]]) for g in rows]
    return jnp.stack([jnp.tile(g.astype(F32), LANES // HEAD_DIM) for g in rows + swapped])


def _encoder_layer(x, tables, prm):
    cos_t, sin_t = tables
    a1, a4, a16, qb, kb, vb = _in_proj(x, prm["attn_norm"], prm["w_in"], cos_t, sin_t, prm["gains"])
    branches = [_dilated_attn(qkv, dilation, (window // 2) // dilation, prm["bias_a"])
                for qkv, (window, dilation) in zip((a1, a4, a16), DILATED_PATTERNS)]
    ob = _swa_attn(qb, kb, vb, prm["sink"], SWA_HALF_WINDOW, prm["bias_b"])
    return _out_ffn(x, [o for o, _ in branches], [l for _, l in branches], ob,
                    prm["w_out"], prm["ffn_norm"], prm["w_gate"], prm["w_up"], prm["w_down"])


def kernel(x_prompt, x_sample, attn_norm, w_in, qnorm_a, knorm_a, qnorm_b, knorm_b, sink_b, w_out,
           ffn_norm, w_gate, w_up, w_down):
    depth = w_in.shape[0]
    order = _swa_head_order()
    half_windows = {(window // 2) // dilation for window, dilation in DILATED_PATTERNS}
    assert len(half_windows) == 1
    bias_a = _band_bias_t(half_windows.pop())
    bias_b = _band_bias_t(SWA_HALF_WINDOW)
    tables = _rope_tables(max(x_prompt.shape[1], x_sample.shape[1]))
    ys = [x_prompt, x_sample]
    for i in range(depth):
        prm = dict(
            attn_norm=attn_norm[i], ffn_norm=ffn_norm[i],
            w_in=_permute_heads(w_in[i], 1, QKV_A, order).astype(BF16),
            w_out=_permute_heads(w_out[i], 0, QA, order).astype(BF16),
            w_gate=w_gate[i].astype(BF16), w_up=w_up[i].astype(BF16), w_down=w_down[i].astype(BF16),
            gains=_gain_rows(qnorm_a[i], knorm_a[i], qnorm_b[i], knorm_b[i]),
            sink=sink_b[i].astype(F32), bias_a=bias_a, bias_b=bias_b)
        ys = [_encoder_layer(y, tables, prm) for y in ys]
    return tuple(ys)
```

```python
import functools
import math

import jax
import jax.numpy as jnp
from jax import lax
from jax.experimental import pallas as pl
from jax.experimental.pallas import tpu as pltpu

HEAD_DIM = 64
N_HEADS_A = 8
N_HEADS_B = 8
N_KV_B = 2
DILATED_PATTERNS = ((128, 1), (512, 4), (2048, 16))
SWA_HALF_WINDOW = 128
ROPE_THETA = 10000.0
EPS = 1e-6

QA = N_HEADS_A * HEAD_DIM
QB = N_HEADS_B * HEAD_DIM
KB = N_KV_B * HEAD_DIM
QKV_A = 3 * QA
LANES = 128
MXU_COLS = 256
Q_BLOCK = 128
TOKEN_TILE = 512
ATTN_TILE = 1024
NEG_BIG = -1e30
VMEM_LIMIT_BYTES = 56 * 1024 * 1024
LOG2_E = math.log2(math.e)

BF16 = jnp.bfloat16
F32 = jnp.float32


def _pick_tile(n, cap):
    t = cap
    while n % t:
        t //= 2
    return t


def _const_spec(shape):
    return pl.BlockSpec(shape, lambda *_: (0,) * len(shape), pipeline_mode=pl.Buffered(1))


def _lane_blocks(n, start=0):
    return [slice(start + i * LANES, start + (i + 1) * LANES) for i in range(n)]


def _software_pipeline(n_items, stages):
    states = [None] * n_items
    for t in range(n_items + len(stages) - 1):
        for s, stage in enumerate(stages):
            if 0 <= t - s < n_items:
                states[t - s] = stage(t - s, states[t - s])


def _in_proj_kernel(x_ref, g_ref, w_ref, cos_ref, sin_ref, gains_ref,
                    a1_ref, a4_ref, a16_ref, qb_ref, kb_ref, vb_ref, stage1_ref, stage4_ref):
    tm = x_ref.shape[0]
    x = x_ref[...]
    ms = jnp.mean(x * x, axis=-1, keepdims=True)
    h = (x * lax.rsqrt(ms + EPS) * g_ref[...]).astype(BF16)

    r = lax.broadcasted_iota(jnp.int32, (MXU_COLS, MXU_COLS), 0)
    c = lax.broadcasted_iota(jnp.int32, (MXU_COLS, MXU_COLS), 1)
    ones_blk = ((r // HEAD_DIM) == (c // HEAD_DIM)).astype(BF16)
    lane = lax.broadcasted_iota(jnp.int32, (tm, LANES), 1)
    first_half = (lane % HEAD_DIM) < (HEAD_DIM // 2)

    cos = cos_ref[...]
    sin = sin_ref[...]
    rope_a = [gains_ref[k:k + 1, :] * cos for k in range(4)]
    rope_b = [gains_ref[4 + k:5 + k, :] * sin for k in range(4)]

    def sum_squares(ps):
        sq = jnp.concatenate([(p * p).astype(BF16) for p in ps], axis=1)
        width = sq.shape[1]
        ss = jnp.dot(sq, ones_blk[:width, :width], preferred_element_type=F32)
        return [ss[:, k * LANES:(k + 1) * LANES] for k in range(len(ps))]

    def norm_rope(p, ss, kind):
        rinv = lax.rsqrt(ss * (1.0 / HEAD_DIM) + EPS)
        swapped = jnp.where(first_half,
                            pltpu.roll(p, LANES - HEAD_DIM // 2, 1),
                            pltpu.roll(p, HEAD_DIM // 2, 1))
        return rinv * (p * rope_a[kind] + swapped * rope_b[kind])

    n_a = QKV_A // LANES
    kinds = [0] * 4 + [1] * 4 + [None] * 4 + [2] * 4 + [3, None]
    b_outs = [(qb_ref, i) for i in range(4)] + [(kb_ref, 0), (vb_ref, 0)]

    def epilogue(blk, val):
        if blk >= n_a:
            out_ref, i = b_outs[blk - n_a]
            out_ref[:, i * LANES:(i + 1) * LANES] = val.astype(BF16)
            return
        a1_ref[:, blk * LANES:(blk + 1) * LANES] = val.astype(BF16)
        stage1_ref[blk] = val
        for r4 in range(4):
            rows4 = stage1_ref[blk, pl.ds(r4, tm // 4, stride=4), :]
            a4_ref[:, r4 * QKV_A + blk * LANES:r4 * QKV_A + (blk + 1) * LANES] = rows4.astype(BF16)
            stage4_ref[blk, r4] = rows4
        for r4 in range(4):
            for j in range(4):
                rows16 = stage4_ref[blk, r4, pl.ds(j, tm // 16, stride=4), :]
                c0 = (r4 + 4 * j) * QKV_A + blk * LANES
                a16_ref[:, c0:c0 + LANES] = rows16.astype(BF16)

    chunk = MXU_COLS
    bounds = [(c0, min(c0 + chunk, w_ref.shape[1])) for c0 in range(0, w_ref.shape[1], chunk)]

    def project(i, _):
        c0, c1 = bounds[i]
        return jnp.dot(h, w_ref[:, c0:c1], preferred_element_type=F32)

    def squares(i, proj):
        c0, c1 = bounds[i]
        blks = list(range(c0 // LANES, c1 // LANES))
        ps = {blk: proj[:, (blk - blks[0]) * LANES:(blk - blks[0] + 1) * LANES] for blk in blks}
        normed = [blk for blk in blks if kinds[blk] is not None]
        ss = dict(zip(normed, sum_squares([ps[blk] for blk in normed]))) if normed else {}
        return ps, ss

    def finish(i, st):
        ps, ss = st
        for blk, p in ps.items():
            epilogue(blk, norm_rope(p, ss[blk], kinds[blk]) if blk in ss else p)

    _software_pipeline(len(bounds), [project, squares, finish])


def _in_proj(x, attn_norm, w_in_bf16, cos_t, sin_t, gains):
    b, s, d = x.shape
    tm = _pick_tile(s, TOKEN_TILE)
    tok = lambda rows, cols: pl.BlockSpec((None, rows, cols), lambda i, n: (i, n, 0))
    tab = pl.BlockSpec((tm, LANES), lambda i, n: (n, 0))
    outs = [(tm // dl, s // dl, dl * QKV_A) for dl in (1, 4, 16)] + [(tm, s, QB), (tm, s, KB), (tm, s, KB)]
    return pl.pallas_call(
        _in_proj_kernel,
        grid=(b, s // tm),
        in_specs=[tok(tm, d), _const_spec((1, d)), _const_spec(w_in_bf16.shape), tab, tab,
                  _const_spec(gains.shape)],
        out_specs=[tok(rows, cols) for rows, _, cols in outs],
        out_shape=[jax.ShapeDtypeStruct((b, ln, cols), BF16) for _, ln, cols in outs],
        scratch_shapes=[pltpu.VMEM((QKV_A // LANES, tm, LANES), F32),
                        pltpu.VMEM((QKV_A // LANES, 4, tm // 4, LANES), F32)],
        compiler_params=pltpu.CompilerParams(
            dimension_semantics=("parallel", "parallel"), vmem_limit_bytes=VMEM_LIMIT_BYTES),
        name="in_proj",
    )(x, attn_norm.reshape(1, d), w_in_bf16, cos_t, sin_t, gains)


def _band_bias_t(w):
    nk = Q_BLOCK + 2 * w
    cc = jnp.arange(nk)[:, None]
    aa = jnp.arange(Q_BLOCK)[None, :]
    band = (cc >= aa) & (cc <= aa + 2 * w)
    variants = [band & ((cc >= w) | (e & 1 == 0)) & ((cc < nk - w) | (e & 2 == 0)) for e in range(4)]
    return jnp.where(jnp.stack(variants), 0.0, NEG_BIG).astype(BF16)


def _window_rows(left_ref, mid_ref, right_ref, j, w, cols):
    tq = mid_ref.shape[0]
    lo, hi = j * Q_BLOCK - w, (j + 1) * Q_BLOCK + w
    parts = []
    if lo < 0:
        parts.append(mid_ref[0:-lo, cols] if left_ref is None else left_ref[w + lo:w, cols])
    parts.append(mid_ref[max(lo, 0):min(hi, tq), cols])
    if hi > tq:
        parts.append(mid_ref[0:hi - tq, cols] if right_ref is None else right_ref[0:hi - tq, cols])
    return parts[0] if len(parts) == 1 else jnp.concatenate(parts, axis=0)


def _mask_variant(n, n_tiles, j, nblk):
    first = (n == 0) & (j == 0)
    last = (n == n_tiles - 1) & (j == nblk - 1)
    return jnp.where(first, 1, 0) + jnp.where(last, 2, 0)


def _attention_items(items, sinks=None):
    bq = Q_BLOCK
    lo_q = lax.broadcasted_iota(jnp.int32, (bq, LANES), 1) < HEAD_DIM
    eye = (lax.broadcasted_iota(jnp.int32, (bq, bq), 0)
           == lax.broadcasted_iota(jnp.int32, (bq, bq), 1)).astype(BF16)
    onehot2 = jnp.concatenate([eye, eye], axis=0)

    def scores(i, _):
        bias_t = items[i]["bias"]()
        out = []
        for q, k in zip(items[i]["q"](), items[i]["k"]()):
            zero = jnp.zeros_like(q)
            q2 = jnp.concatenate([jnp.where(lo_q, q, zero), jnp.where(lo_q, zero, q)], axis=0)
            lhs = jnp.concatenate([q2, onehot2], axis=1)
            k_aug = jnp.concatenate([k, bias_t], axis=1)
            out.append(lax.dot_general(lhs, k_aug, (((1,), (1,)), ((), ())), preferred_element_type=F32))
        return out

    def softmax(i, sc):
        maxes = [jnp.max(s, axis=-1, keepdims=True) for s in sc]
        if sinks is not None:
            upper = lax.broadcasted_iota(jnp.int32, (2 * bq, 1), 0) >= bq
            maxes = [jnp.maximum(m, jnp.where(upper, so, se)) for m, (se, so) in zip(maxes, sinks)]
        return maxes, [jnp.exp2(s - m).astype(BF16) for s, m in zip(sc, maxes)]

    def weighted_values(i, st):
        maxes, probs = st
        res, augmented = [], {}
        for p, v in zip(probs, items[i]["v"]()):
            if id(v) not in augmented:
                lo_v = lax.broadcasted_iota(jnp.int32, v.shape, 1) < HEAD_DIM
                one = jnp.ones_like(v)
                augmented[id(v)] = (jnp.where(lo_v, v, one), jnp.where(lo_v, one, v))
            v_lo, v_hi = augmented[id(v)]
            res.append((jnp.dot(p[:bq], v_lo, preferred_element_type=F32),
                        jnp.dot(p[bq:], v_hi, preferred_element_type=F32)))
        return maxes, res

    def finish(i, st):
        maxes, res = st
        outs = []
        for k, ((re, ro), m) in enumerate(zip(res, maxes)):
            acc = jnp.where(lo_q, re, ro)
            l = pltpu.roll(jnp.where(lo_q, ro, re), HEAD_DIM, 1)
            m_rep = jnp.where(lo_q, m[:bq], m[bq:])
            if sinks is not None:
                l = l + jnp.exp2(jnp.where(lo_q, sinks[k][0], sinks[k][1]) - m_rep)
            outs.append((acc / l, m_rep + jnp.log2(l)))
        items[i]["store"](outs)
        return None

    _software_pipeline(len(items), [scores, softmax, weighted_values, finish])


def _dilated_attn_kernel(mid_ref, *refs, w, n_tiles, n_res):
    left_ref, right_ref = refs[:-3] if n_tiles > 1 else (None, None)
    bias_ref, o_ref, lse_ref = refs[-3:]
    tq = mid_ref.shape[0]
    nblk = tq // Q_BLOCK
    n = pl.program_id(2)
    n_pairs = N_HEADS_A // 2

    def item(res, j):
        rows = slice(j * Q_BLOCK, (j + 1) * Q_BLOCK)
        base = res * QKV_A

        def store(outs):
            for cs, (o, lse) in zip(_lane_blocks(n_pairs, res * QA), outs):
                o_ref[rows, cs] = o.astype(BF16)
                lse_ref[rows, cs] = lse

        window = lambda c0: [_window_rows(left_ref, mid_ref, right_ref, j, w, cs)
                             for cs in _lane_blocks(n_pairs, base + c0)]
        return dict(q=lambda: [mid_ref[rows, cs] for cs in _lane_blocks(n_pairs, base)],
                    k=lambda: window(QA), v=lambda: window(2 * QA),
                    bias=lambda: bias_ref[_mask_variant(n, n_tiles, j, nblk)], store=store)

    _attention_items([item(res, j) for res in range(n_res) for j in range(nblk)])


def _dilated_attn(qkv, dilation, half_window, bias_t):
    b, ln, _ = qkv.shape
    w = half_window
    tq = _pick_tile(ln, ATTN_TILE)
    n_res = min(dilation, ATTN_TILE // tq)
    n_tiles = ln // tq
    per = tq // w
    last_halo = ln // w - 1
    in_cols = n_res * QKV_A
    halo_specs = [
        pl.BlockSpec((None, w, in_cols), lambda i, r, n: (i, jnp.maximum(n * per - 1, 0), r)),
        pl.BlockSpec((None, w, in_cols), lambda i, r, n: (i, jnp.minimum((n + 1) * per, last_halo), r)),
    ] if n_tiles > 1 else []
    return pl.pallas_call(
        functools.partial(_dilated_attn_kernel, w=w, n_tiles=n_tiles, n_res=n_res),
        grid=(b, dilation // n_res, n_tiles),
        in_specs=([pl.BlockSpec((None, tq, in_cols), lambda i, r, n: (i, n, r))] + halo_specs
                  + [_const_spec(bias_t.shape)]),
        out_specs=[pl.BlockSpec((None, tq, n_res * QA), lambda i, r, n: (i, n, r))] * 2,
        out_shape=[jax.ShapeDtypeStruct((b, ln, dilation * QA), BF16),
                   jax.ShapeDtypeStruct((b, ln, dilation * QA), F32)],
        compiler_params=pltpu.CompilerParams(
            dimension_semantics=("parallel", "parallel", "parallel"),
            vmem_limit_bytes=VMEM_LIMIT_BYTES),
        name=f"dilated_attn_d{dilation}",
    )(*([qkv] * (1 + len(halo_specs))), bias_t)


def _swa_kernel(sink_ref, q_ref, km_ref, kl_ref, kr_ref, vm_ref, vl_ref, vr_ref, bias_ref, o_ref,
                *, w, n_tiles):
    tq = q_ref.shape[0]
    nblk = tq // Q_BLOCK
    n = pl.program_id(1)
    group = N_HEADS_B // N_KV_B
    all_lanes = slice(0, LANES)
    sinks = [(sink_ref[p] * LOG2_E, sink_ref[group + p] * LOG2_E) for p in range(group)]

    def item(j):
        rows = slice(j * Q_BLOCK, (j + 1) * Q_BLOCK)

        def store(outs):
            for cs, (o, _) in zip(_lane_blocks(group), outs):
                o_ref[rows, cs] = o.astype(BF16)

        return dict(q=lambda: [q_ref[rows, cs] for cs in _lane_blocks(group)],
                    k=lambda: [_window_rows(kl_ref, km_ref, kr_ref, j, w, all_lanes)] * group,
                    v=lambda: [_window_rows(vl_ref, vm_ref, vr_ref, j, w, all_lanes)] * group,
                    bias=lambda: bias_ref[_mask_variant(n, n_tiles, j, nblk)], store=store)

    _attention_items([item(j) for j in range(nblk)], sinks)


def _swa_attn(q, k, v, sink, half_window, bias_t):
    b, s, cols = q.shape
    w = half_window
    tq = _pick_tile(s, ATTN_TILE)
    n_tiles = s // tq
    per = tq // w
    last_halo = s // w - 1
    kv_specs = [pl.BlockSpec((None, tq, KB), lambda i, n: (i, n, 0)),
                pl.BlockSpec((None, w, KB), lambda i, n: (i, jnp.maximum(n * per - 1, 0), 0)),
                pl.BlockSpec((None, w, KB), lambda i, n: (i, jnp.minimum((n + 1) * per, last_halo), 0))]
    return pl.pallas_call(
        functools.partial(_swa_kernel, w=w, n_tiles=n_tiles),
        grid=(b, n_tiles),
        in_specs=[pl.BlockSpec(memory_space=pltpu.SMEM),
                  pl.BlockSpec((None, tq, cols), lambda i, n: (i, n, 0))] + kv_specs + kv_specs
                 + [_const_spec(bias_t.shape)],
        out_specs=pl.BlockSpec((None, tq, cols), lambda i, n: (i, n, 0)),
        out_shape=jax.ShapeDtypeStruct((b, s, cols), BF16),
        compiler_params=pltpu.CompilerParams(
            dimension_semantics=("parallel", "parallel"), vmem_limit_bytes=VMEM_LIMIT_BYTES),
        name="swa_attn",
    )(sink, q, k, k, k, v, v, v, bias_t)


def _out_ffn_kernel(x_ref, o1_ref, o4_ref, o16_ref, l1_ref, l4_ref, l16_ref, ob_ref,
                    wo_ref, g_ref, wg_ref, wu_ref, wd_ref, y_ref,
                    x1_ref, act_ref, lnat_ref, onat_ref, *, ff_chunk):
    tm = x_ref.shape[0]
    for slot, (d, l_ref, o_ref) in enumerate(((4, l4_ref, o4_ref), (16, l16_ref, o16_ref))):
        for res in range(d):
            rows = pl.ds(res, tm // d, stride=d)
            for blk, cs in enumerate(_lane_blocks(QA // LANES, res * QA)):
                lnat_ref[slot, blk, rows, :] = l_ref[:, cs]
                onat_ref[slot, blk, rows, :] = o_ref[:, cs].astype(F32)

    x1_b = x_ref[...] + jnp.dot(ob_ref[...], wo_ref[QA:QA + QB, :], preferred_element_type=F32)

    gather = lambda ref, slot: jnp.concatenate([ref[slot, blk] for blk in range(QA // LANES)], axis=1)
    l1, l2, l3 = l1_ref[...], gather(lnat_ref, 0), gather(lnat_ref, 1)
    mx = jnp.maximum(jnp.maximum(l1, l2), l3)
    e1, e2, e3 = jnp.exp2(l1 - mx), jnp.exp2(l2 - mx), jnp.exp2(l3 - mx)
    mixed = e1 * o1_ref[...].astype(F32) + e2 * gather(onat_ref, 0) + e3 * gather(onat_ref, 1)
    out_a = (mixed / (e1 + e2 + e3)).astype(BF16)
    x1 = x1_b + jnp.dot(out_a, wo_ref[0:QA, :], preferred_element_type=F32)
    ms = jnp.mean(x1 * x1, axis=-1, keepdims=True)
    h2 = (x1 * lax.rsqrt(ms + EPS) * g_ref[...]).astype(BF16)
    x1_ref[...] = x1

    def gate_up(i, _):
        cs = slice(i * ff_chunk, (i + 1) * ff_chunk)
        return (jnp.dot(h2, wg_ref[:, cs], preferred_element_type=F32),
                jnp.dot(h2, wu_ref[:, cs], preferred_element_type=F32))

    def swiglu(i, gu):
        gate, up = gu
        act_ref[:, i * ff_chunk:(i + 1) * ff_chunk] = (gate / (1.0 + jnp.exp(-gate)) * up).astype(BF16)

    _software_pipeline(wg_ref.shape[1] // ff_chunk, [gate_up, swiglu])
    y_ref[...] = x1_ref[...] + jnp.dot(act_ref[...], wd_ref[...], preferred_element_type=F32)


def _out_ffn(x, o_branches, lse_branches, ob, wo, ffn_norm, wg, wu, wd):
    b, s, d = x.shape
    tm = _pick_tile(s, TOKEN_TILE)
    tok = lambda rows, cols: pl.BlockSpec((None, rows, cols), lambda i, n: (i, n, 0))
    dils = [dl for _, dl in DILATED_PATTERNS]
    consts = (wo, ffn_norm.reshape(1, d), wg, wu, wd)
    return pl.pallas_call(
        functools.partial(_out_ffn_kernel, ff_chunk=MXU_COLS),
        grid=(b, s // tm),
        in_specs=([tok(tm, d)] + [tok(tm // dl, dl * QA) for dl in dils] * 2 + [tok(tm, QB)]
                  + [_const_spec(t.shape) for t in consts]),
        out_specs=tok(tm, d),
        out_shape=jax.ShapeDtypeStruct((b, s, d), F32),
        scratch_shapes=[pltpu.VMEM((tm, d), F32), pltpu.VMEM((tm, wg.shape[1]), BF16),
                        pltpu.VMEM((2, QA // LANES, tm, LANES), F32),
                        pltpu.VMEM((2, QA // LANES, tm, LANES), F32)],
        compiler_params=pltpu.CompilerParams(
            dimension_semantics=("parallel", "parallel"), vmem_limit_bytes=VMEM_LIMIT_BYTES),
        name="out_ffn",
    )(x, *o_branches, *lse_branches, ob, *consts)


def _swa_head_order():
    group = N_HEADS_B // N_KV_B
    return [h for p in range(group) for h in (p, group + p)]


def _permute_heads(w, axis, start, heads):
    take = lambda lo, hi: lax.slice_in_dim(w, lo, hi, axis=axis)
    parts = ([take(0, start)]
             + [take(start + h * HEAD_DIM, start + (h + 1) * HEAD_DIM) for h in heads]
             + [take(start + len(heads) * HEAD_DIM, w.shape[axis])])
    return jnp.concatenate(parts, axis=axis)


def _rope_tables(s):
    half = HEAD_DIM // 2
    inv = ROPE_THETA ** (-2.0 * jnp.arange(half, dtype=F32) / HEAD_DIM)
    inv_lanes = jnp.tile(inv, LANES // half)
    sign = jnp.tile(jnp.concatenate([-jnp.ones(half, F32), jnp.ones(half, F32)]), LANES // HEAD_DIM)
    ang = jnp.arange(s).astype(F32)[:, None] * inv_lanes[None, :]
    return jnp.cos(ang), jnp.sin(ang) * sign[None, :]


def _gain_rows(qnorm_a, knorm_a, qnorm_b, knorm_b):
    half = HEAD_DIM // 2
    scale = HEAD_DIM ** -0.5 * LOG2_E
    rows = [qnorm_a * scale, knorm_a, qnorm_b * scale, knorm_b]
    swapped = [jnp.concatenate([g[half:], g[:half]]) for g in rows]
    return jnp.stack([jnp.tile(g.astype(F32), LANES // HEAD_DIM) for g in rows + swapped])


def _encoder_layer(x, tables, prm):
    cos_t, sin_t = tables
    a1, a4, a16, qb, kb, vb = _in_proj(x, prm["attn_norm"], prm["w_in"], cos_t, sin_t, prm["gains"])
    branches = [_dilated_attn(qkv, dilation, (window // 2) // dilation, prm["bias_a"])
                for qkv, (window, dilation) in zip((a1, a4, a16), DILATED_PATTERNS)]
    ob = _swa_attn(qb, kb, vb, prm["sink"], SWA_HALF_WINDOW, prm["bias_b"])
    return _out_ffn(x, [o for o, _ in branches], [l for _, l in branches], ob,
                    prm["w_out"], prm["ffn_norm"], prm["w_gate"], prm["w_up"], prm["w_down"])


def kernel(x_prompt, x_sample, attn_norm, w_in, qnorm_a, knorm_a, qnorm_b, knorm_b, sink_b, w_out,
           ffn_norm, w_gate, w_up, w_down):
    depth = w_in.shape[0]
    order = _swa_head_order()
    half_windows = {(window // 2) // dilation for window, dilation in DILATED_PATTERNS}
    assert len(half_windows) == 1
    bias_a = _band_bias_t(half_windows.pop())
    bias_b = _band_bias_t(SWA_HALF_WINDOW)
    tables = _rope_tables(max(x_prompt.shape[1], x_sample.shape[1]))
    ys = [x_prompt, x_sample]
    for i in range(depth):
        prm = dict(
            attn_norm=attn_norm[i], ffn_norm=ffn_norm[i],
            w_in=_permute_heads(w_in[i], 1, QKV_A, order).astype(BF16),
            w_out=_permute_heads(w_out[i], 0, QA, order).astype(BF16),
            w_gate=w_gate[i].astype(BF16), w_up=w_up[i].astype(BF16), w_down=w_down[i].astype(BF16),
            gains=_gain_rows(qnorm_a[i], knorm_a[i], qnorm_b[i], knorm_b[i]),
            sink=sink_b[i].astype(F32), bias_a=bias_a, bias_b=bias_b)
        ys = [_encoder_layer(y, tables, prm) for y in ys]
    return tuple(ys)
```

```python
import functools
import math

import jax
import jax.numpy as jnp
from jax import lax
from jax.experimental import pallas as pl
from jax.experimental.pallas import tpu as pltpu

HEAD_DIM = 64
N_HEADS_A = 8
N_HEADS_B = 8
N_KV_B = 2
DILATED_PATTERNS = ((128, 1), (512, 4), (2048, 16))
SWA_HALF_WINDOW = 128
ROPE_THETA = 10000.0
EPS = 1e-6

QA = N_HEADS_A * HEAD_DIM
QB = N_HEADS_B * HEAD_DIM
KB = N_KV_B * HEAD_DIM
QKV_A = 3 * QA
LANES = 128
MXU_COLS = 256
N_MXU = 2
Q_BLOCK = 128
TOKEN_TILE = 512
PROJ_TILE = 1024
ATTN_TILE = 1024
SWA_TILE = 2048
NEG_BIG = -1e30
VMEM_LIMIT_BYTES = 56 * 1024 * 1024
LOG2_E = math.log2(math.e)

BF16 = jnp.bfloat16
F32 = jnp.float32


def _pick_tile(n, cap):
    t = cap
    while n % t:
        t //= 2
    return t


def _const_spec(shape):
    return pl.BlockSpec(shape, lambda *_: (0,) * len(shape), pipeline_mode=pl.Buffered(1))


def _lane_blocks(n, start=0):
    return [slice(start + i * LANES, start + (i + 1) * LANES) for i in range(n)]


def _software_pipeline(n_items, stages):
    states = [None] * n_items
    for t in range(n_items + len(stages) - 1):
        for s, stage in enumerate(stages):
            if 0 <= t - s < n_items:
                states[t - s] = stage(t - s, states[t - s])


def _in_proj_kernel(x_ref, g_ref, w_ref, cos_ref, sin_ref, gains_ref,
                    a1_ref, a4_ref, a16_ref, qb_ref, kb_ref, vb_ref, stage1_ref, stage4_ref):
    tm = x_ref.shape[0]
    x = x_ref[...]
    ms = jnp.mean(x * x, axis=-1, keepdims=True)
    h = (x * lax.rsqrt(ms + EPS) * g_ref[...]).astype(BF16)

    r = lax.broadcasted_iota(jnp.int32, (MXU_COLS, MXU_COLS), 0)
    c = lax.broadcasted_iota(jnp.int32, (MXU_COLS, MXU_COLS), 1)
    ones_blk = ((r // HEAD_DIM) == (c // HEAD_DIM)).astype(BF16)
    lane = lax.broadcasted_iota(jnp.int32, (tm, LANES), 1)
    first_half = (lane % HEAD_DIM) < (HEAD_DIM // 2)

    cos = cos_ref[...]
    sin = sin_ref[...]
    rope_a = [gains_ref[k:k + 1, :] * cos for k in range(4)]
    rope_b = [gains_ref[4 + k:5 + k, :] * sin for k in range(4)]

    def sum_squares(ps):
        sq = jnp.concatenate([(p * p).astype(BF16) for p in ps], axis=1)
        width = sq.shape[1]
        ss = jnp.dot(sq, ones_blk[:width, :width], preferred_element_type=F32)
        return [ss[:, k * LANES:(k + 1) * LANES] for k in range(len(ps))]

    def norm_rope(p, ss, kind):
        rinv = lax.rsqrt(ss * (1.0 / HEAD_DIM) + EPS)
        swapped = jnp.where(first_half,
                            pltpu.roll(p, LANES - HEAD_DIM // 2, 1),
                            pltpu.roll(p, HEAD_DIM // 2, 1))
        return rinv * (p * rope_a[kind] + swapped * rope_b[kind])

    n_a = QKV_A // LANES
    kinds = [0] * 4 + [1] * 4 + [None] * 4 + [2] * 4 + [3, None]
    b_outs = [(qb_ref, i) for i in range(4)] + [(kb_ref, 0), (vb_ref, 0)]

    def epilogue(blk, val):
        if blk >= n_a:
            out_ref, i = b_outs[blk - n_a]
            out_ref[:, i * LANES:(i + 1) * LANES] = val.astype(BF16)
            return
        a1_ref[:, blk * LANES:(blk + 1) * LANES] = val.astype(BF16)
        stage1_ref[blk] = val
        for r4 in range(4):
            rows4 = stage1_ref[blk, pl.ds(r4, tm // 4, stride=4), :]
            a4_ref[:, r4 * QKV_A + blk * LANES:r4 * QKV_A + (blk + 1) * LANES] = rows4.astype(BF16)
            stage4_ref[blk, r4] = rows4
        for r4 in range(4):
            for j in range(4):
                rows16 = stage4_ref[blk, r4, pl.ds(j, tm // 16, stride=4), :]
                c0 = (r4 + 4 * j) * QKV_A + blk * LANES
                a16_ref[:, c0:c0 + LANES] = rows16.astype(BF16)

    chunk = MXU_COLS
    bounds = [(c0, min(c0 + chunk, w_ref.shape[1])) for c0 in range(0, w_ref.shape[1], chunk)]

    def project(i, _):
        c0, c1 = bounds[i]
        return jnp.dot(h, w_ref[:, c0:c1], preferred_element_type=F32)

    def finish(i, proj):
        c0, c1 = bounds[i]
        blks = list(range(c0 // LANES, c1 // LANES))
        ps = {blk: proj[:, (blk - blks[0]) * LANES:(blk - blks[0] + 1) * LANES] for blk in blks}
        normed = [blk for blk in blks if kinds[blk] is not None]
        per_pass = MXU_COLS // LANES
        for g in range(0, len(normed), per_pass):
            grp = normed[g:g + per_pass]
            for blk, ss in zip(grp, sum_squares([ps[blk] for blk in grp])):
                ps[blk] = norm_rope(ps[blk], ss, kinds[blk])
        for blk in blks:
            epilogue(blk, ps[blk])

    _software_pipeline(len(bounds), [project, finish])


def _in_proj(x, attn_norm, w_in_bf16, cos_t, sin_t, gains):
    b, s, d = x.shape
    tm = _pick_tile(s, PROJ_TILE)
    tok = lambda rows, cols: pl.BlockSpec((None, rows, cols), lambda i, n: (i, n, 0))
    tab = pl.BlockSpec((tm, LANES), lambda i, n: (n, 0))
    outs = [(tm // dl, s // dl, dl * QKV_A) for dl in (1, 4, 16)] + [(tm, s, QB), (tm, s, KB), (tm, s, KB)]
    return pl.pallas_call(
        _in_proj_kernel,
        grid=(b, s // tm),
        in_specs=[tok(tm, d), _const_spec((1, d)), _const_spec(w_in_bf16.shape), tab, tab,
                  _const_spec(gains.shape)],
        out_specs=[tok(rows, cols) for rows, _, cols in outs],
        out_shape=[jax.ShapeDtypeStruct((b, ln, cols), BF16) for _, ln, cols in outs],
        scratch_shapes=[pltpu.VMEM((QKV_A // LANES, tm, LANES), F32),
                        pltpu.VMEM((QKV_A // LANES, 4, tm // 4, LANES), F32)],
        compiler_params=pltpu.CompilerParams(
            dimension_semantics=("parallel", "parallel"), vmem_limit_bytes=VMEM_LIMIT_BYTES),
        name="in_proj",
    )(x, attn_norm.reshape(1, d), w_in_bf16, cos_t, sin_t, gains)


def _band_bias_t(w):
    nk = Q_BLOCK + 2 * w
    cc = jnp.arange(nk)[:, None]
    aa = jnp.arange(Q_BLOCK)[None, :]
    band = (cc >= aa) & (cc <= aa + 2 * w)
    variants = [band & ((cc >= w) | (e & 1 == 0)) & ((cc < nk - w) | (e & 2 == 0)) for e in range(4)]
    return jnp.where(jnp.stack(variants), 0.0, NEG_BIG).astype(BF16)


def _window_rows(left_ref, mid_ref, right_ref, j, w, cols):
    tq = mid_ref.shape[0]
    lo, hi = j * Q_BLOCK - w, (j + 1) * Q_BLOCK + w
    parts = []
    if lo < 0:
        parts.append(mid_ref[0:-lo, cols] if left_ref is None else left_ref[w + lo:w, cols])
    parts.append(mid_ref[max(lo, 0):min(hi, tq), cols])
    if hi > tq:
        parts.append(mid_ref[0:hi - tq, cols] if right_ref is None else right_ref[0:hi - tq, cols])
    return parts[0] if len(parts) == 1 else jnp.concatenate(parts, axis=0)


def _mask_variant(n, n_tiles, j, nblk):
    first = (n == 0) & (j == 0)
    last = (n == n_tiles - 1) & (j == nblk - 1)
    return jnp.where(first, 1, 0) + jnp.where(last, 2, 0)


def _attention_items(items, sinks=None):
    bq = Q_BLOCK
    lo_q = lax.broadcasted_iota(jnp.int32, (bq, LANES), 1) < HEAD_DIM
    eye = (lax.broadcasted_iota(jnp.int32, (bq, bq), 0)
           == lax.broadcasted_iota(jnp.int32, (bq, bq), 1)).astype(BF16)
    onehot2 = jnp.concatenate([eye, eye], axis=0)

    def scores(i, _):
        bias_t = items[i]["bias"]()
        out = []
        for q, k in zip(items[i]["q"](), items[i]["k"]()):
            zero = jnp.zeros_like(q)
            q2 = jnp.concatenate([jnp.where(lo_q, q, zero), jnp.where(lo_q, zero, q)], axis=0)
            lhs = jnp.concatenate([q2, onehot2], axis=1)
            k_aug = jnp.concatenate([k, bias_t], axis=1)
            out.append(lax.dot_general(lhs, k_aug, (((1,), (1,)), ((), ())), preferred_element_type=F32))
        return out

    def softmax(i, sc):
        maxes = [jnp.max(s, axis=-1, keepdims=True) for s in sc]
        if sinks is not None:
            upper = lax.broadcasted_iota(jnp.int32, (2 * bq, 1), 0) >= bq
            maxes = [jnp.maximum(m, jnp.where(upper, so, se)) for m, (se, so) in zip(maxes, sinks)]
        return maxes, [jnp.exp2(s - m).astype(BF16) for s, m in zip(sc, maxes)]

    def weighted_values(i, st):
        maxes, probs = st
        res, augmented = [], {}
        for p, v in zip(probs, items[i]["v"]()):
            if id(v) not in augmented:
                lo_v = lax.broadcasted_iota(jnp.int32, v.shape, 1) < HEAD_DIM
                one = jnp.ones_like(v)
                augmented[id(v)] = (jnp.where(lo_v, v, one), jnp.where(lo_v, one, v))
            v_lo, v_hi = augmented[id(v)]
            res.append((jnp.dot(p[:bq], v_lo, preferred_element_type=F32),
                        jnp.dot(p[bq:], v_hi, preferred_element_type=F32)))
        return maxes, res

    def finish(i, st):
        maxes, res = st
        outs = []
        for k, ((re, ro), m) in enumerate(zip(res, maxes)):
            acc = jnp.where(lo_q, re, ro)
            l = pltpu.roll(jnp.where(lo_q, ro, re), HEAD_DIM, 1)
            m_rep = jnp.where(lo_q, m[:bq], m[bq:])
            if sinks is not None:
                l = l + jnp.exp2(jnp.where(lo_q, sinks[k][0], sinks[k][1]) - m_rep)
            outs.append((acc / l, m_rep + jnp.log2(l)))
        items[i]["store"](outs)
        return None

    _software_pipeline(len(items), [scores, softmax, weighted_values, finish])


def _dilated_attn_kernel(mid_ref, *refs, w, n_tiles, n_res):
    left_ref, right_ref = refs[:-3] if n_tiles > 1 else (None, None)
    bias_ref, o_ref, lse_ref = refs[-3:]
    tq = mid_ref.shape[0]
    nblk = tq // Q_BLOCK
    n = pl.program_id(2)
    n_pairs = N_HEADS_A // 2

    def item(res, j):
        rows = slice(j * Q_BLOCK, (j + 1) * Q_BLOCK)
        base = res * QKV_A

        def store(outs):
            for cs, (o, lse) in zip(_lane_blocks(n_pairs, res * QA), outs):
                o_ref[rows, cs] = o.astype(BF16)
                lse_ref[rows, cs] = lse

        window = lambda c0: [_window_rows(left_ref, mid_ref, right_ref, j, w, cs)
                             for cs in _lane_blocks(n_pairs, base + c0)]
        return dict(q=lambda: [mid_ref[rows, cs] for cs in _lane_blocks(n_pairs, base)],
                    k=lambda: window(QA), v=lambda: window(2 * QA),
                    bias=lambda: bias_ref[_mask_variant(n, n_tiles, j, nblk)], store=store)

    _attention_items([item(res, j) for res in range(n_res) for j in range(nblk)])


def _dilated_attn(qkv, dilation, half_window, bias_t):
    b, ln, _ = qkv.shape
    w = half_window
    tq = _pick_tile(ln, ATTN_TILE)
    n_res = min(dilation, ATTN_TILE // tq)
    n_tiles = ln // tq
    per = tq // w
    last_halo = ln // w - 1
    in_cols = n_res * QKV_A
    halo_specs = [
        pl.BlockSpec((None, w, in_cols), lambda i, r, n: (i, jnp.maximum(n * per - 1, 0), r)),
        pl.BlockSpec((None, w, in_cols), lambda i, r, n: (i, jnp.minimum((n + 1) * per, last_halo), r)),
    ] if n_tiles > 1 else []
    return pl.pallas_call(
        functools.partial(_dilated_attn_kernel, w=w, n_tiles=n_tiles, n_res=n_res),
        grid=(b, dilation // n_res, n_tiles),
        in_specs=([pl.BlockSpec((None, tq, in_cols), lambda i, r, n: (i, n, r))] + halo_specs
                  + [_const_spec(bias_t.shape)]),
        out_specs=[pl.BlockSpec((None, tq, n_res * QA), lambda i, r, n: (i, n, r))] * 2,
        out_shape=[jax.ShapeDtypeStruct((b, ln, dilation * QA), BF16),
                   jax.ShapeDtypeStruct((b, ln, dilation * QA), F32)],
        compiler_params=pltpu.CompilerParams(
            dimension_semantics=("parallel", "parallel", "parallel"),
            vmem_limit_bytes=VMEM_LIMIT_BYTES),
        name=f"dilated_attn_d{dilation}",
    )(*([qkv] * (1 + len(halo_specs))), bias_t)


def _swa_kernel(sink_ref, q_ref, km_ref, kl_ref, kr_ref, vm_ref, vl_ref, vr_ref, bias_ref, o_ref,
                *, w, n_tiles):
    tq = q_ref.shape[0]
    nblk = tq // Q_BLOCK
    n = pl.program_id(1)
    group = N_HEADS_B // N_KV_B
    all_lanes = slice(0, LANES)
    sinks = [(sink_ref[p] * LOG2_E, sink_ref[group + p] * LOG2_E) for p in range(group)]

    def item(j):
        rows = slice(j * Q_BLOCK, (j + 1) * Q_BLOCK)

        def store(outs):
            for cs, (o, _) in zip(_lane_blocks(group), outs):
                o_ref[rows, cs] = o.astype(BF16)

        return dict(q=lambda: [q_ref[rows, cs] for cs in _lane_blocks(group)],
                    k=lambda: [_window_rows(kl_ref, km_ref, kr_ref, j, w, all_lanes)] * group,
                    v=lambda: [_window_rows(vl_ref, vm_ref, vr_ref, j, w, all_lanes)] * group,
                    bias=lambda: bias_ref[_mask_variant(n, n_tiles, j, nblk)], store=store)

    _attention_items([item(j) for j in range(nblk)], sinks)


def _swa_attn(q, k, v, sink, half_window, bias_t):
    b, s, cols = q.shape
    w = half_window
    tq = _pick_tile(s, SWA_TILE)
    n_tiles = s // tq
    per = tq // w
    last_halo = s // w - 1
    kv_specs = [pl.BlockSpec((None, tq, KB), lambda i, n: (i, n, 0)),
                pl.BlockSpec((None, w, KB), lambda i, n: (i, jnp.maximum(n * per - 1, 0), 0)),
                pl.BlockSpec((None, w, KB), lambda i, n: (i, jnp.minimum((n + 1) * per, last_halo), 0))]
    return pl.pallas_call(
        functools.partial(_swa_kernel, w=w, n_tiles=n_tiles),
        grid=(b, n_tiles),
        in_specs=[pl.BlockSpec(memory_space=pltpu.SMEM),
                  pl.BlockSpec((None, tq, cols), lambda i, n: (i, n, 0))] + kv_specs + kv_specs
                 + [_const_spec(bias_t.shape)],
        out_specs=pl.BlockSpec((None, tq, cols), lambda i, n: (i, n, 0)),
        out_shape=jax.ShapeDtypeStruct((b, s, cols), BF16),
        compiler_params=pltpu.CompilerParams(
            dimension_semantics=("parallel", "parallel"), vmem_limit_bytes=VMEM_LIMIT_BYTES),
        name="swa_attn",
    )(sink, q, k, k, k, v, v, v, bias_t)


def _out_ffn_kernel(x_ref, o1_ref, o4_ref, o16_ref, l1_ref, l4_ref, l16_ref, ob_ref,
                    wo_ref, g_ref, wg_ref, wu_ref, wd_ref, y_ref,
                    x1_ref, act_ref, lnat_ref, onat_ref, *, ff_chunk):
    tm = x_ref.shape[0]
    for slot, (d, l_ref, o_ref) in enumerate(((4, l4_ref, o4_ref), (16, l16_ref, o16_ref))):
        for res in range(d):
            rows = pl.ds(res, tm // d, stride=d)
            for blk, cs in enumerate(_lane_blocks(QA // LANES, res * QA)):
                lnat_ref[slot, blk, rows, :] = l_ref[:, cs]
                onat_ref[slot, blk, rows, :] = o_ref[:, cs].astype(F32)

    gather = lambda ref, slot: jnp.concatenate([ref[slot, blk] for blk in range(QA // LANES)], axis=1)
    l1, l2, l3 = l1_ref[...], gather(lnat_ref, 0), gather(lnat_ref, 1)
    mx = jnp.maximum(jnp.maximum(l1, l2), l3)
    e1, e2, e3 = jnp.exp2(l1 - mx), jnp.exp2(l2 - mx), jnp.exp2(l3 - mx)
    mixed = e1 * o1_ref[...].astype(F32) + e2 * gather(onat_ref, 0) + e3 * gather(onat_ref, 1)
    out_a = (mixed / (e1 + e2 + e3)).astype(BF16)
    x1 = (x_ref[...]
          + jnp.dot(out_a, wo_ref[0:QA, :], preferred_element_type=F32)
          + jnp.dot(ob_ref[...], wo_ref[QA:QA + QB, :], preferred_element_type=F32))
    ms = jnp.mean(x1 * x1, axis=-1, keepdims=True)
    h2 = (x1 * lax.rsqrt(ms + EPS) * g_ref[...]).astype(BF16)
    x1_ref[...] = x1

    def gate_up(i, _):
        cs = slice(i * ff_chunk, (i + 1) * ff_chunk)
        return (jnp.dot(h2, wg_ref[:, cs], preferred_element_type=F32),
                jnp.dot(h2, wu_ref[:, cs], preferred_element_type=F32))

    def swiglu(i, gu):
        gate, up = gu
        act_ref[:, i * ff_chunk:(i + 1) * ff_chunk] = (gate / (1.0 + jnp.exp(-gate)) * up).astype(BF16)

    _software_pipeline(wg_ref.shape[1] // ff_chunk, [gate_up, swiglu])
    y_ref[...] = x1_ref[...] + jnp.dot(act_ref[...], wd_ref[...], preferred_element_type=F32)


def _out_ffn(x, o_branches, lse_branches, ob, wo, ffn_norm, wg, wu, wd):
    b, s, d = x.shape
    tm = _pick_tile(s, TOKEN_TILE)
    tok = lambda rows, cols: pl.BlockSpec((None, rows, cols), lambda i, n: (i, n, 0))
    dils = [dl for _, dl in DILATED_PATTERNS]
    consts = (wo, ffn_norm.reshape(1, d), wg, wu, wd)
    return pl.pallas_call(
        functools.partial(_out_ffn_kernel, ff_chunk=MXU_COLS),
        grid=(b, s // tm),
        in_specs=([tok(tm, d)] + [tok(tm // dl, dl * QA) for dl in dils] * 2 + [tok(tm, QB)]
                  + [_const_spec(t.shape) for t in consts]),
        out_specs=tok(tm, d),
        out_shape=jax.ShapeDtypeStruct((b, s, d), F32),
        scratch_shapes=[pltpu.VMEM((tm, d), F32), pltpu.VMEM((tm, wg.shape[1]), BF16),
                        pltpu.VMEM((2, QA // LANES, tm, LANES), F32),
                        pltpu.VMEM((2, QA // LANES, tm, LANES), F32)],
        compiler_params=pltpu.CompilerParams(
            dimension_semantics=("parallel", "parallel"), vmem_limit_bytes=VMEM_LIMIT_BYTES),
        name="out_ffn",
    )(x, *o_branches, *lse_branches, ob, *consts)


def _swa_head_order():
    group = N_HEADS_B // N_KV_B
    return [h for p in range(group) for h in (p, group + p)]


def _permute_heads(w, axis, start, heads):
    take = lambda lo, hi: lax.slice_in_dim(w, lo, hi, axis=axis)
    parts = ([take(0, start)]
             + [take(start + h * HEAD_DIM, start + (h + 1) * HEAD_DIM) for h in heads]
             + [take(start + len(heads) * HEAD_DIM, w.shape[axis])])
    return jnp.concatenate(parts, axis=axis)


def _rope_tables(s):
    half = HEAD_DIM // 2
    inv = ROPE_THETA ** (-2.0 * jnp.arange(half, dtype=F32) / HEAD_DIM)
    inv_lanes = jnp.tile(inv, LANES // half)
    sign = jnp.tile(jnp.concatenate([-jnp.ones(half, F32), jnp.ones(half, F32)]), LANES // HEAD_DIM)
    ang = jnp.arange(s).astype(F32)[:, None] * inv_lanes[None, :]
    return jnp.cos(ang), jnp.sin(ang) * sign[None, :]


def _gain_rows(qnorm_a, knorm_a, qnorm_b, knorm_b):
    half = HEAD_DIM // 2
    scale = HEAD_DIM ** -0.5 * LOG2_E
    rows = [qnorm_a * scale, knorm_a, qnorm_b * scale, knorm_b]
    swapped = [jnp.concatenate([g[half:], g[:half]]) for g in rows]
    return jnp.stack([jnp.tile(g.astype(F32), LANES // HEAD_DIM) for g in rows + swapped])


def _encoder_layer(x, tables, prm):
    cos_t, sin_t = tables
    a1, a4, a16, qb, kb, vb = _in_proj(x, prm["attn_norm"], prm["w_in"], cos_t, sin_t, prm["gains"])
    branches = [_dilated_attn(qkv, dilation, (window // 2) // dilation, prm["bias_a"])
                for qkv, (window, dilation) in zip((a1, a4, a16), DILATED_PATTERNS)]
    ob = _swa_attn(qb, kb, vb, prm["sink"], SWA_HALF_WINDOW, prm["bias_b"])
    return _out_ffn(x, [o for o, _ in branches], [l for _, l in branches], ob,
                    prm["w_out"], prm["ffn_norm"], prm["w_gate"], prm["w_up"], prm["w_down"])


def kernel(x_prompt, x_sample, attn_norm, w_in, qnorm_a, knorm_a, qnorm_b, knorm_b, sink_b, w_out,
           ffn_norm, w_gate, w_up, w_down):
    depth = w_in.shape[0]
    order = _swa_head_order()
    half_windows = {(window // 2) // dilation for window, dilation in DILATED_PATTERNS}
    assert len(half_windows) == 1
    bias_a = _band_bias_t(half_windows.pop())
    bias_b = _band_bias_t(SWA_HALF_WINDOW)
    tables = _rope_tables(max(x_prompt.shape[1], x_sample.shape[1]))
    ys = [x_prompt, x_sample]
    for i in range(depth):
        prm = dict(
            attn_norm=attn_norm[i], ffn_norm=ffn_norm[i],
            w_in=_permute_heads(w_in[i], 1, QKV_A, order).astype(BF16),
            w_out=_permute_heads(w_out[i], 0, QA, order).astype(BF16),
            w_gate=w_gate[i].astype(BF16), w_up=w_up[i].astype(BF16), w_down=w_down[i].astype(BF16),
            gains=_gain_rows(qnorm_a[i], knorm_a[i], qnorm_b[i], knorm_b[i]),
            sink=sink_b[i].astype(F32), bias_a=bias_a, bias_b=bias_b)
        ys = [_encoder_layer(y, tables, prm) for y in ys]
    return tuple(ys)
```

```python
import functools
import math

import jax
import jax.numpy as jnp
from jax import lax
from jax.experimental import pallas as pl
from jax.experimental.pallas import tpu as pltpu

HEAD_DIM = 64
N_HEADS_A = 8
N_HEADS_B = 8
N_KV_B = 2
DILATED_PATTERNS = ((128, 1), (512, 4), (2048, 16))
SWA_HALF_WINDOW = 128
ROPE_THETA = 10000.0
EPS = 1e-6

QA = N_HEADS_A * HEAD_DIM
QB = N_HEADS_B * HEAD_DIM
KB = N_KV_B * HEAD_DIM
QKV_A = 3 * QA
LANES = 128
MXU_COLS = 256
N_MXU = 2
Q_BLOCK = 128
TOKEN_TILE = 512
ATTN_TILE = 1024
NEG_BIG = -1e30
VMEM_LIMIT_BYTES = 56 * 1024 * 1024
LOG2_E = math.log2(math.e)

BF16 = jnp.bfloat16
F32 = jnp.float32


def _pick_tile(n, cap):
    t = cap
    while n % t:
        t //= 2
    return t


def _const_spec(shape):
    return pl.BlockSpec(shape, lambda *_: (0,) * len(shape), pipeline_mode=pl.Buffered(1))


def _lane_blocks(n, start=0):
    return [slice(start + i * LANES, start + (i + 1) * LANES) for i in range(n)]


def _software_pipeline(n_items, stages):
    states = [None] * n_items
    for t in range(n_items + len(stages) - 1):
        for s, stage in enumerate(stages):
            if 0 <= t - s < n_items:
                states[t - s] = stage(t - s, states[t - s])


def _in_proj_kernel(x_ref, g_ref, w_ref, cos_ref, sin_ref, gains_ref,
                    a1_ref, a4_ref, a16_ref, qb_ref, kb_ref, vb_ref, stage1_ref, stage4_ref):
    tm = x_ref.shape[0]
    x = x_ref[...]
    ms = jnp.mean(x * x, axis=-1, keepdims=True)
    h = (x * lax.rsqrt(ms + EPS) * g_ref[...]).astype(BF16)

    r = lax.broadcasted_iota(jnp.int32, (MXU_COLS, MXU_COLS), 0)
    c = lax.broadcasted_iota(jnp.int32, (MXU_COLS, MXU_COLS), 1)
    ones_blk = ((r // HEAD_DIM) == (c // HEAD_DIM)).astype(BF16)
    lane = lax.broadcasted_iota(jnp.int32, (tm, LANES), 1)
    first_half = (lane % HEAD_DIM) < (HEAD_DIM // 2)

    cos = cos_ref[...]
    sin = sin_ref[...]
    rope_a = [gains_ref[k:k + 1, :] * cos for k in range(4)]
    rope_b = [gains_ref[4 + k:5 + k, :] * sin for k in range(4)]

    def sum_squares(ps):
        sq = jnp.concatenate([(p * p).astype(BF16) for p in ps], axis=1)
        width = sq.shape[1]
        ss = jnp.dot(sq, ones_blk[:width, :width], preferred_element_type=F32)
        return [ss[:, k * LANES:(k + 1) * LANES] for k in range(len(ps))]

    def norm_rope(p, ss, kind):
        rinv = lax.rsqrt(ss * (1.0 / HEAD_DIM) + EPS)
        swapped = jnp.where(first_half,
                            pltpu.roll(p, LANES - HEAD_DIM // 2, 1),
                            pltpu.roll(p, HEAD_DIM // 2, 1))
        return rinv * (p * rope_a[kind] + swapped * rope_b[kind])

    n_a = QKV_A // LANES
    kinds = [0] * 4 + [1] * 4 + [None] * 4 + [2] * 4 + [3, None]
    b_outs = [(qb_ref, i) for i in range(4)] + [(kb_ref, 0), (vb_ref, 0)]

    def epilogue(blk, val):
        if blk >= n_a:
            out_ref, i = b_outs[blk - n_a]
            out_ref[:, i * LANES:(i + 1) * LANES] = val.astype(BF16)
            return
        a1_ref[:, blk * LANES:(blk + 1) * LANES] = val.astype(BF16)
        stage1_ref[blk] = val
        for r4 in range(4):
            rows4 = stage1_ref[blk, pl.ds(r4, tm // 4, stride=4), :]
            a4_ref[:, r4 * QKV_A + blk * LANES:r4 * QKV_A + (blk + 1) * LANES] = rows4.astype(BF16)
            stage4_ref[blk, r4] = rows4
        for r4 in range(4):
            for j in range(4):
                rows16 = stage4_ref[blk, r4, pl.ds(j, tm // 16, stride=4), :]
                c0 = (r4 + 4 * j) * QKV_A + blk * LANES
                a16_ref[:, c0:c0 + LANES] = rows16.astype(BF16)

    chunk = MXU_COLS
    bounds = [(c0, min(c0 + chunk, w_ref.shape[1])) for c0 in range(0, w_ref.shape[1], chunk)]

    def project(i, _):
        c0, c1 = bounds[i]
        return jnp.dot(h, w_ref[:, c0:c1], preferred_element_type=F32)

    def finish(i, proj):
        c0, c1 = bounds[i]
        blks = list(range(c0 // LANES, c1 // LANES))
        ps = {blk: proj[:, (blk - blks[0]) * LANES:(blk - blks[0] + 1) * LANES] for blk in blks}
        normed = [blk for blk in blks if kinds[blk] is not None]
        per_pass = MXU_COLS // LANES
        for g in range(0, len(normed), per_pass):
            grp = normed[g:g + per_pass]
            for blk, ss in zip(grp, sum_squares([ps[blk] for blk in grp])):
                ps[blk] = norm_rope(ps[blk], ss, kinds[blk])
        for blk in blks:
            epilogue(blk, ps[blk])

    _software_pipeline(len(bounds), [project, finish])


def _in_proj(x, attn_norm, w_in_bf16, cos_t, sin_t, gains):
    b, s, d = x.shape
    tm = _pick_tile(s, TOKEN_TILE)
    tok = lambda rows, cols: pl.BlockSpec((None, rows, cols), lambda i, n: (i, n, 0))
    tab = pl.BlockSpec((tm, LANES), lambda i, n: (n, 0))
    outs = [(tm // dl, s // dl, dl * QKV_A) for dl in (1, 4, 16)] + [(tm, s, QB), (tm, s, KB), (tm, s, KB)]
    return pl.pallas_call(
        _in_proj_kernel,
        grid=(b, s // tm),
        in_specs=[tok(tm, d), _const_spec((1, d)), _const_spec(w_in_bf16.shape), tab, tab,
                  _const_spec(gains.shape)],
        out_specs=[tok(rows, cols) for rows, _, cols in outs],
        out_shape=[jax.ShapeDtypeStruct((b, ln, cols), BF16) for _, ln, cols in outs],
        scratch_shapes=[pltpu.VMEM((QKV_A // LANES, tm, LANES), F32),
                        pltpu.VMEM((QKV_A // LANES, 4, tm // 4, LANES), F32)],
        compiler_params=pltpu.CompilerParams(
            dimension_semantics=("parallel", "parallel"), vmem_limit_bytes=VMEM_LIMIT_BYTES),
        name="in_proj",
    )(x, attn_norm.reshape(1, d), w_in_bf16, cos_t, sin_t, gains)


def _band_bias_t(w):
    nk = Q_BLOCK + 2 * w
    cc = jnp.arange(nk)[:, None]
    aa = jnp.arange(Q_BLOCK)[None, :]
    band = (cc >= aa) & (cc <= aa + 2 * w)
    variants = [band & ((cc >= w) | (e & 1 == 0)) & ((cc < nk - w) | (e & 2 == 0)) for e in range(4)]
    return jnp.where(jnp.stack(variants), 0.0, NEG_BIG).astype(BF16)


def _window_rows(left_ref, mid_ref, right_ref, j, w, cols):
    tq = mid_ref.shape[0]
    lo, hi = j * Q_BLOCK - w, (j + 1) * Q_BLOCK + w
    parts = []
    if lo < 0:
        parts.append(mid_ref[0:-lo, cols] if left_ref is None else left_ref[w + lo:w, cols])
    parts.append(mid_ref[max(lo, 0):min(hi, tq), cols])
    if hi > tq:
        parts.append(mid_ref[0:hi - tq, cols] if right_ref is None else right_ref[0:hi - tq, cols])
    return parts[0] if len(parts) == 1 else jnp.concatenate(parts, axis=0)


def _mask_variant(n, n_tiles, j, nblk):
    first = (n == 0) & (j == 0)
    last = (n == n_tiles - 1) & (j == nblk - 1)
    return jnp.where(first, 1, 0) + jnp.where(last, 2, 0)


def _attention_items(items, all_sinks=None):
    bq = Q_BLOCK
    lo_q = lax.broadcasted_iota(jnp.int32, (bq, LANES), 1) < HEAD_DIM
    eye = (lax.broadcasted_iota(jnp.int32, (bq, bq), 0)
           == lax.broadcasted_iota(jnp.int32, (bq, bq), 1)).astype(BF16)
    onehot2 = jnp.concatenate([eye, eye], axis=0)

    def scores(i, _):
        bias_t = items[i]["bias"]()
        out = []
        for q, k in zip(items[i]["q"](), items[i]["k"]()):
            zero = jnp.zeros_like(q)
            q2 = jnp.concatenate([jnp.where(lo_q, q, zero), jnp.where(lo_q, zero, q)], axis=0)
            lhs = jnp.concatenate([q2, onehot2], axis=1)
            k_aug = jnp.concatenate([k, bias_t], axis=1)
            out.append(lax.dot_general(lhs, k_aug, (((1,), (1,)), ((), ())), preferred_element_type=F32))
        return out

    def softmax(i, sc):
        sinks = items[i].get("sinks", all_sinks)
        maxes = [jnp.max(s, axis=-1, keepdims=True) for s in sc]
        if sinks is not None:
            upper = lax.broadcasted_iota(jnp.int32, (2 * bq, 1), 0) >= bq
            maxes = [jnp.maximum(m, jnp.where(upper, so, se)) for m, (se, so) in zip(maxes, sinks)]
        return maxes, [jnp.exp2(s - m).astype(BF16) for s, m in zip(sc, maxes)]

    def weighted_values(i, st):
        maxes, probs = st
        res, augmented = [], {}
        for p, v in zip(probs, items[i]["v"]()):
            if id(v) not in augmented:
                lo_v = lax.broadcasted_iota(jnp.int32, v.shape, 1) < HEAD_DIM
                one = jnp.ones_like(v)
                augmented[id(v)] = (jnp.where(lo_v, v, one), jnp.where(lo_v, one, v))
            v_lo, v_hi = augmented[id(v)]
            res.append((jnp.dot(p[:bq], v_lo, preferred_element_type=F32),
                        jnp.dot(p[bq:], v_hi, preferred_element_type=F32)))
        return maxes, res

    def finish(i, st):
        sinks = items[i].get("sinks", all_sinks)
        maxes, res = st
        outs = []
        for k, ((re, ro), m) in enumerate(zip(res, maxes)):
            acc = jnp.where(lo_q, re, ro)
            l = pltpu.roll(jnp.where(lo_q, ro, re), HEAD_DIM, 1)
            m_rep = jnp.where(lo_q, m[:bq], m[bq:])
            if sinks is not None:
                l = l + jnp.exp2(jnp.where(lo_q, sinks[k][0], sinks[k][1]) - m_rep)
            outs.append((acc / l, m_rep + jnp.log2(l)))
        items[i]["store"](outs)
        return None

    _software_pipeline(len(items), [scores, softmax, weighted_values, finish])


def _dilated_attn_kernel(mid_ref, *refs, w, n_tiles, n_res):
    left_ref, right_ref = refs[:-3] if n_tiles > 1 else (None, None)
    bias_ref, o_ref, lse_ref = refs[-3:]
    tq = mid_ref.shape[0]
    nblk = tq // Q_BLOCK
    n = pl.program_id(2)
    n_pairs = N_HEADS_A // 2

    def item(res, j):
        rows = slice(j * Q_BLOCK, (j + 1) * Q_BLOCK)
        base = res * QKV_A

        def store(outs):
            for cs, (o, lse) in zip(_lane_blocks(n_pairs, res * QA), outs):
                o_ref[rows, cs] = o.astype(BF16)
                lse_ref[rows, cs] = lse

        window = lambda c0: [_window_rows(left_ref, mid_ref, right_ref, j, w, cs)
                             for cs in _lane_blocks(n_pairs, base + c0)]
        return dict(q=lambda: [mid_ref[rows, cs] for cs in _lane_blocks(n_pairs, base)],
                    k=lambda: window(QA), v=lambda: window(2 * QA),
                    bias=lambda: bias_ref[_mask_variant(n, n_tiles, j, nblk)], store=store)

    _attention_items([item(res, j) for res in range(n_res) for j in range(nblk)])


def _dilated_attn(qkv, dilation, half_window, bias_t):
    b, ln, _ = qkv.shape
    w = half_window
    tq = _pick_tile(ln, ATTN_TILE)
    n_res = min(dilation, ATTN_TILE // tq)
    n_tiles = ln // tq
    per = tq // w
    last_halo = ln // w - 1
    in_cols = n_res * QKV_A
    halo_specs = [
        pl.BlockSpec((None, w, in_cols), lambda i, r, n: (i, jnp.maximum(n * per - 1, 0), r)),
        pl.BlockSpec((None, w, in_cols), lambda i, r, n: (i, jnp.minimum((n + 1) * per, last_halo), r)),
    ] if n_tiles > 1 else []
    return pl.pallas_call(
        functools.partial(_dilated_attn_kernel, w=w, n_tiles=n_tiles, n_res=n_res),
        grid=(b, dilation // n_res, n_tiles),
        in_specs=([pl.BlockSpec((None, tq, in_cols), lambda i, r, n: (i, n, r))] + halo_specs
                  + [_const_spec(bias_t.shape)]),
        out_specs=[pl.BlockSpec((None, tq, n_res * QA), lambda i, r, n: (i, n, r))] * 2,
        out_shape=[jax.ShapeDtypeStruct((b, ln, dilation * QA), BF16),
                   jax.ShapeDtypeStruct((b, ln, dilation * QA), F32)],
        compiler_params=pltpu.CompilerParams(
            dimension_semantics=("parallel", "parallel", "parallel"),
            vmem_limit_bytes=VMEM_LIMIT_BYTES),
        name=f"dilated_attn_d{dilation}",
    )(*([qkv] * (1 + len(halo_specs))), bias_t)


def _swa_kernel(sink_ref, q_ref, km_ref, kl_ref, kr_ref, vm_ref, vl_ref, vr_ref, bias_ref, o_ref,
                *, w, n_tiles):
    tq = q_ref.shape[0]
    nblk = tq // Q_BLOCK
    n = pl.program_id(1)
    group = N_HEADS_B // N_KV_B
    all_lanes = slice(0, LANES)
    sinks = [(sink_ref[p] * LOG2_E, sink_ref[group + p] * LOG2_E) for p in range(group)]

    def item(j):
        rows = slice(j * Q_BLOCK, (j + 1) * Q_BLOCK)

        def store(outs):
            for cs, (o, _) in zip(_lane_blocks(group), outs):
                o_ref[rows, cs] = o.astype(BF16)

        return dict(q=lambda: [q_ref[rows, cs] for cs in _lane_blocks(group)],
                    k=lambda: [_window_rows(kl_ref, km_ref, kr_ref, j, w, all_lanes)] * group,
                    v=lambda: [_window_rows(vl_ref, vm_ref, vr_ref, j, w, all_lanes)] * group,
                    bias=lambda: bias_ref[_mask_variant(n, n_tiles, j, nblk)], store=store)

    _attention_items([item(j) for j in range(nblk)], sinks)


def _swa_attn(q, k, v, sink, half_window, bias_t):
    b, s, cols = q.shape
    w = half_window
    tq = _pick_tile(s, ATTN_TILE)
    n_tiles = s // tq
    per = tq // w
    last_halo = s // w - 1
    kv_specs = [pl.BlockSpec((None, tq, KB), lambda i, n: (i, n, 0)),
                pl.BlockSpec((None, w, KB), lambda i, n: (i, jnp.maximum(n * per - 1, 0), 0)),
                pl.BlockSpec((None, w, KB), lambda i, n: (i, jnp.minimum((n + 1) * per, last_halo), 0))]
    return pl.pallas_call(
        functools.partial(_swa_kernel, w=w, n_tiles=n_tiles),
        grid=(b, n_tiles),
        in_specs=[pl.BlockSpec(memory_space=pltpu.SMEM),
                  pl.BlockSpec((None, tq, cols), lambda i, n: (i, n, 0))] + kv_specs + kv_specs
                 + [_const_spec(bias_t.shape)],
        out_specs=pl.BlockSpec((None, tq, cols), lambda i, n: (i, n, 0)),
        out_shape=jax.ShapeDtypeStruct((b, s, cols), BF16),
        compiler_params=pltpu.CompilerParams(
            dimension_semantics=("parallel", "parallel"), vmem_limit_bytes=VMEM_LIMIT_BYTES),
        name="swa_attn",
    )(sink, q, k, k, k, v, v, v, bias_t)


def _natural_attn_kernel(sink_ref, a_mid, a_left, a_right, q_ref, km_ref, kl_ref, kr_ref,
                         vm_ref, vl_ref, vr_ref, bias_a_ref, bias_b_ref, oa_ref, lse_ref, ob_ref,
                         *, wa, wb, n_tiles):
    tq = q_ref.shape[0]
    nblk = tq // Q_BLOCK
    n = pl.program_id(1)
    group = N_HEADS_B // N_KV_B
    n_pairs = N_HEADS_A // 2
    all_lanes = slice(0, LANES)
    sinks = [(sink_ref[p] * LOG2_E, sink_ref[group + p] * LOG2_E) for p in range(group)]

    def dilated_item(j):
        rows = slice(j * Q_BLOCK, (j + 1) * Q_BLOCK)

        def store(outs):
            for cs, (o, lse) in zip(_lane_blocks(n_pairs), outs):
                oa_ref[rows, cs] = o.astype(BF16)
                lse_ref[rows, cs] = lse

        window = lambda c0: [_window_rows(a_left, a_mid, a_right, j, wa, cs) for cs in _lane_blocks(n_pairs, c0)]
        return dict(q=lambda: [a_mid[rows, cs] for cs in _lane_blocks(n_pairs)],
                    k=lambda: window(QA), v=lambda: window(2 * QA),
                    bias=lambda: bias_a_ref[_mask_variant(n, n_tiles, j, nblk)], store=store, sinks=None)

    def swa_item(j):
        rows = slice(j * Q_BLOCK, (j + 1) * Q_BLOCK)

        def store(outs):
            for cs, (o, _) in zip(_lane_blocks(group), outs):
                ob_ref[rows, cs] = o.astype(BF16)

        return dict(q=lambda: [q_ref[rows, cs] for cs in _lane_blocks(group)],
                    k=lambda: [_window_rows(kl_ref, km_ref, kr_ref, j, wb, all_lanes)] * group,
                    v=lambda: [_window_rows(vl_ref, vm_ref, vr_ref, j, wb, all_lanes)] * group,
                    bias=lambda: bias_b_ref[_mask_variant(n, n_tiles, j, nblk)], store=store, sinks=sinks)

    _attention_items([it for j in range(nblk) for it in (swa_item(j), dilated_item(j))])


def _natural_attn(a1, qb, kb, vb, sink, wa, wb, bias_a, bias_b):
    b, s, _ = a1.shape
    tq = _pick_tile(s, ATTN_TILE)
    n_tiles = s // tq
    assert n_tiles > 1

    def window_specs(w, cols):
        per, last = tq // w, s // w - 1
        return [pl.BlockSpec((None, tq, cols), lambda i, n: (i, n, 0)),
                pl.BlockSpec((None, w, cols), lambda i, n: (i, jnp.maximum(n * per - 1, 0), 0)),
                pl.BlockSpec((None, w, cols), lambda i, n: (i, jnp.minimum((n + 1) * per, last), 0))]

    kv_specs = window_specs(wb, KB)
    tile = lambda cols: pl.BlockSpec((None, tq, cols), lambda i, n: (i, n, 0))
    return pl.pallas_call(
        functools.partial(_natural_attn_kernel, wa=wa, wb=wb, n_tiles=n_tiles),
        grid=(b, n_tiles),
        in_specs=([pl.BlockSpec(memory_space=pltpu.SMEM)] + window_specs(wa, QKV_A) + [tile(QB)]
                  + kv_specs + kv_specs + [_const_spec(bias_a.shape), _const_spec(bias_b.shape)]),
        out_specs=[tile(QA), tile(QA), tile(QB)],
        out_shape=[jax.ShapeDtypeStruct((b, s, QA), BF16), jax.ShapeDtypeStruct((b, s, QA), F32),
                   jax.ShapeDtypeStruct((b, s, QB), BF16)],
        compiler_params=pltpu.CompilerParams(
            dimension_semantics=("parallel", "parallel"), vmem_limit_bytes=VMEM_LIMIT_BYTES),
        name="natural_attn",
    )(sink, a1, a1, a1, qb, kb, kb, kb, vb, vb, vb, bias_a, bias_b)


def _out_ffn_kernel(x_ref, o1_ref, o4_ref, o16_ref, l1_ref, l4_ref, l16_ref, ob_ref,
                    wo_ref, g_ref, wg_ref, wu_ref, wd_ref, y_ref,
                    x1_ref, act_ref, lnat_ref, onat_ref, *, ff_chunk):
    tm = x_ref.shape[0]
    for slot, (d, l_ref, o_ref) in enumerate(((4, l4_ref, o4_ref), (16, l16_ref, o16_ref))):
        for res in range(d):
            rows = pl.ds(res, tm // d, stride=d)
            for blk, cs in enumerate(_lane_blocks(QA // LANES, res * QA)):
                lnat_ref[slot, blk, rows, :] = l_ref[:, cs]
                onat_ref[slot, blk, rows, :] = o_ref[:, cs].astype(F32)

    gather = lambda ref, slot: jnp.concatenate([ref[slot, blk] for blk in range(QA // LANES)], axis=1)
    l1, l2, l3 = l1_ref[...], gather(lnat_ref, 0), gather(lnat_ref, 1)
    mx = jnp.maximum(jnp.maximum(l1, l2), l3)
    e1, e2, e3 = jnp.exp2(l1 - mx), jnp.exp2(l2 - mx), jnp.exp2(l3 - mx)
    mixed = e1 * o1_ref[...].astype(F32) + e2 * gather(onat_ref, 0) + e3 * gather(onat_ref, 1)
    out_a = (mixed / (e1 + e2 + e3)).astype(BF16)
    x1 = (x_ref[...]
          + jnp.dot(out_a, wo_ref[0:QA, :], preferred_element_type=F32)
          + jnp.dot(ob_ref[...], wo_ref[QA:QA + QB, :], preferred_element_type=F32))
    ms = jnp.mean(x1 * x1, axis=-1, keepdims=True)
    h2 = (x1 * lax.rsqrt(ms + EPS) * g_ref[...]).astype(BF16)
    x1_ref[...] = x1

    def gate_up(i, _):
        cs = slice(i * ff_chunk, (i + 1) * ff_chunk)
        return (jnp.dot(h2, wg_ref[:, cs], preferred_element_type=F32),
                jnp.dot(h2, wu_ref[:, cs], preferred_element_type=F32))

    def swiglu(i, gu):
        gate, up = gu
        act_ref[:, i * ff_chunk:(i + 1) * ff_chunk] = (gate / (1.0 + jnp.exp(-gate)) * up).astype(BF16)

    _software_pipeline(wg_ref.shape[1] // ff_chunk, [gate_up, swiglu])
    y_ref[...] = x1_ref[...] + jnp.dot(act_ref[...], wd_ref[...], preferred_element_type=F32)


def _out_ffn(x, o_branches, lse_branches, ob, wo, ffn_norm, wg, wu, wd):
    b, s, d = x.shape
    tm = _pick_tile(s, TOKEN_TILE)
    tok = lambda rows, cols: pl.BlockSpec((None, rows, cols), lambda i, n: (i, n, 0))
    dils = [dl for _, dl in DILATED_PATTERNS]
    consts = (wo, ffn_norm.reshape(1, d), wg, wu, wd)
    return pl.pallas_call(
        functools.partial(_out_ffn_kernel, ff_chunk=MXU_COLS),
        grid=(b, s // tm),
        in_specs=([tok(tm, d)] + [tok(tm // dl, dl * QA) for dl in dils] * 2 + [tok(tm, QB)]
                  + [_const_spec(t.shape) for t in consts]),
        out_specs=tok(tm, d),
        out_shape=jax.ShapeDtypeStruct((b, s, d), F32),
        scratch_shapes=[pltpu.VMEM((tm, d), F32), pltpu.VMEM((tm, wg.shape[1]), BF16),
                        pltpu.VMEM((2, QA // LANES, tm, LANES), F32),
                        pltpu.VMEM((2, QA // LANES, tm, LANES), F32)],
        compiler_params=pltpu.CompilerParams(
            dimension_semantics=("parallel", "parallel"), vmem_limit_bytes=VMEM_LIMIT_BYTES),
        name="out_ffn",
    )(x, *o_branches, *lse_branches, ob, *consts)


def _swa_head_order():
    group = N_HEADS_B // N_KV_B
    return [h for p in range(group) for h in (p, group + p)]


def _permute_heads(w, axis, start, heads):
    take = lambda lo, hi: lax.slice_in_dim(w, lo, hi, axis=axis)
    parts = ([take(0, start)]
             + [take(start + h * HEAD_DIM, start + (h + 1) * HEAD_DIM) for h in heads]
             + [take(start + len(heads) * HEAD_DIM, w.shape[axis])])
    return jnp.concatenate(parts, axis=axis)


def _rope_tables(s):
    half = HEAD_DIM // 2
    inv = ROPE_THETA ** (-2.0 * jnp.arange(half, dtype=F32) / HEAD_DIM)
    inv_lanes = jnp.tile(inv, LANES // half)
    sign = jnp.tile(jnp.concatenate([-jnp.ones(half, F32), jnp.ones(half, F32)]), LANES // HEAD_DIM)
    ang = jnp.arange(s).astype(F32)[:, None] * inv_lanes[None, :]
    return jnp.cos(ang), jnp.sin(ang) * sign[None, :]


def _gain_rows(qnorm_a, knorm_a, qnorm_b, knorm_b):
    half = HEAD_DIM // 2
    scale = HEAD_DIM ** -0.5 * LOG2_E
    rows = [qnorm_a * scale, knorm_a, qnorm_b * scale, knorm_b]
    swapped = [jnp.concatenate([g[half:], g[:half]]) for g in rows]
    return jnp.stack([jnp.tile(g.astype(F32), LANES // HEAD_DIM) for g in rows + swapped])


def _encoder_layer(x, tables, prm):
    cos_t, sin_t = tables
    a1, a4, a16, qb, kb, vb = _in_proj(x, prm["attn_norm"], prm["w_in"], cos_t, sin_t, prm["gains"])
    (window1, dilation1), *dilated = DILATED_PATTERNS
    assert dilation1 == 1
    o1, l1, ob = _natural_attn(a1, qb, kb, vb, prm["sink"], window1 // 2, SWA_HALF_WINDOW,
                               prm["bias_a"], prm["bias_b"])
    branches = [(o1, l1)] + [_dilated_attn(qkv, dilation, (window // 2) // dilation, prm["bias_a"])
                             for qkv, (window, dilation) in zip((a4, a16), dilated)]
    return _out_ffn(x, [o for o, _ in branches], [l for _, l in branches], ob,
                    prm["w_out"], prm["ffn_norm"], prm["w_gate"], prm["w_up"], prm["w_down"])


def kernel(x_prompt, x_sample, attn_norm, w_in, qnorm_a, knorm_a, qnorm_b, knorm_b, sink_b, w_out,
           ffn_norm, w_gate, w_up, w_down):
    depth = w_in.shape[0]
    order = _swa_head_order()
    half_windows = {(window // 2) // dilation for window, dilation in DILATED_PATTERNS}
    assert len(half_windows) == 1
    bias_a = _band_bias_t(half_windows.pop())
    bias_b = _band_bias_t(SWA_HALF_WINDOW)
    tables = _rope_tables(max(x_prompt.shape[1], x_sample.shape[1]))
    ys = [x_prompt, x_sample]
    for i in range(depth):
        prm = dict(
            attn_norm=attn_norm[i], ffn_norm=ffn_norm[i],
            w_in=_permute_heads(w_in[i], 1, QKV_A, order).astype(BF16),
            w_out=_permute_heads(w_out[i], 0, QA, order).astype(BF16),
            w_gate=w_gate[i].astype(BF16), w_up=w_up[i].astype(BF16), w_down=w_down[i].astype(BF16),
            gains=_gain_rows(qnorm_a[i], knorm_a[i], qnorm_b[i], knorm_b[i]),
            sink=sink_b[i].astype(F32), bias_a=bias_a, bias_b=bias_b)
        ys = [_encoder_layer(y, tables, prm) for y in ys]
    return tuple(ys)
```
